```python
import math
import jax, jax.numpy as jnp
from jax import lax
import numpy as np

D_MODEL = 1024
BATCH = 4
SEQ = 4096
DEPTH = 1
DEC_BATCH = 16
DEC_SEQ = 32
PAST_LEN = 4096

CHUNK = 64
N_META = 16
HEAD_DIM = 64
SB_HEADS = 8
FOX_HEADS = 8
SB_WIDTH = SB_HEADS * HEAD_DIM
FOX_WIDTH = FOX_HEADS * HEAD_DIM
MIX_WIDTH = SB_WIDTH + FOX_WIDTH
IN_WIDTH = 4 * SB_WIDTH + 4 * FOX_WIDTH + FOX_HEADS
Q_BLOCK = 128
EPS = 1e-6

kernel_name = 'hymba_stickbreak_fox_stream_step'


def rmsnorm(x, g):
    xf = x.astype(jnp.float32)
    y = xf * lax.rsqrt(jnp.mean(xf * xf, axis=-1, keepdims=True) + EPS)
    return (y * g.astype(jnp.float32)).astype(x.dtype)


def project(h, g_norm, w_in, b_f):
    B, T, _ = h.shape
    u = rmsnorm(h, g_norm) @ w_in
    s, f = SB_WIDTH, FOX_WIDTH
    qa, ka, va, ga, qb, kb, vb, gb, fl = jnp.split(
        u, [s, 2 * s, 3 * s, 4 * s, 4 * s + f, 4 * s + 2 * f, 4 * s + 3 * f, 4 * s + 4 * f], axis=-1)
    heads = lambda t, n: t.reshape(B, T, n, HEAD_DIM)
    logf = jax.nn.log_sigmoid(fl.astype(jnp.float32) + b_f.astype(jnp.float32))
    return (heads(qa, SB_HEADS), heads(ka, SB_HEADS), heads(va, SB_HEADS), ga,
            heads(qb, FOX_HEADS), heads(kb, FOX_HEADS), heads(vb, FOX_HEADS), gb, logf)


def combine(h, oa, ga, ob, gb, w_out):
    B, T, _ = h.shape
    mixed = jnp.concatenate([oa.reshape(B, T, SB_WIDTH) * jax.nn.silu(ga),
                             ob.reshape(B, T, FOX_WIDTH) * jax.nn.silu(gb)], axis=-1)
    return h + mixed @ w_out


def sb_block(q, k, v, q_start):
    Tq, Tk = q.shape[1], k.shape[1]
    z = jnp.einsum('bqhd,bkhd->bhqk', q, k).astype(jnp.float32) * (1.0 / math.sqrt(HEAD_DIM))
    t_idx = q_start + jnp.arange(Tq)[:, None]
    s_idx = jnp.arange(Tk)[None, :]
    before = s_idx < t_idx
    log_keep = jnp.where(before, jax.nn.log_sigmoid(-z), 0.0)
    later = lax.cumsum(log_keep, axis=3, reverse=True) - log_keep
    w = jnp.where(before, jnp.exp(jax.nn.log_sigmoid(z) + later), 0.0)
    return jnp.einsum('bhqk,bkhd->bqhd', w.astype(v.dtype), v)


def fox_block(q, k, v, c, q_start):
    Tq, Tk = q.shape[1], k.shape[1]
    logits = jnp.einsum('bqhd,bkhd->bhqk', q, k).astype(jnp.float32) * (1.0 / math.sqrt(HEAD_DIM))
    c_q = jnp.transpose(c[:, q_start:q_start + Tq], (0, 2, 1))
    c_k = jnp.transpose(c, (0, 2, 1))
    decay = c_q[:, :, :, None] - c_k[:, :, None, :]
    causal = jnp.arange(Tk)[None, :] <= (q_start + jnp.arange(Tq)[:, None])
    p = jax.nn.softmax(jnp.where(causal, logits + decay, -jnp.inf), axis=-1)
    return jnp.einsum('bhqk,bkhd->bqhd', p.astype(v.dtype), v)


def attend_prompt(qa, ka, va, qb, kb, vb, logf):
    L = qa.shape[1]
    c = jnp.cumsum(logf, axis=1)
    oa, ob = [], []
    for start in range(0, L, Q_BLOCK):
        end = min(start + Q_BLOCK, L)
        oa.append(sb_block(qa[:, start:end], ka[:, :end], va[:, :end], start))
        ob.append(fox_block(qb[:, start:end], kb[:, :end], vb[:, :end], c[:, :end], start))
    return jnp.concatenate(oa, axis=1), jnp.concatenate(ob, axis=1)


def attend_sample(qa, ka, va, qb, kb, vb, logf, ck_a, cv_a, ck_b, cv_b, clogf):
    P = ck_a.shape[1]
    ka_all = jnp.concatenate([ck_a.astype(ka.dtype), ka], axis=1)
    va_all = jnp.concatenate([cv_a.astype(va.dtype), va], axis=1)
    kb_all = jnp.concatenate([ck_b.astype(kb.dtype), kb], axis=1)
    vb_all = jnp.concatenate([cv_b.astype(vb.dtype), vb], axis=1)
    c = jnp.cumsum(jnp.concatenate([clogf.astype(jnp.float32), logf], axis=1), axis=1)
    return sb_block(qa, ka_all, va_all, P), fox_block(qb, kb_all, vb_all, c, P)


def setup_inputs(seed: int = 0) -> dict:
    key = jax.random.key(seed)
    ks = jax.random.split(key, 14)
    nrm = jax.random.normal
    return {
        'x_prompt': nrm(ks[0], (BATCH, SEQ, D_MODEL), jnp.float32),
        'x_sample': nrm(ks[1], (DEC_BATCH, DEC_SEQ, D_MODEL), jnp.float32),
        'cache_a_k': nrm(ks[2], (DEPTH, DEC_BATCH, PAST_LEN, SB_HEADS, HEAD_DIM), jnp.float32),
        'cache_a_v': nrm(ks[3], (DEPTH, DEC_BATCH, PAST_LEN, SB_HEADS, HEAD_DIM), jnp.float32),
        'cache_b_k': nrm(ks[4], (DEPTH, DEC_BATCH, PAST_LEN, FOX_HEADS, HEAD_DIM), jnp.float32),
        'cache_b_v': nrm(ks[5], (DEPTH, DEC_BATCH, PAST_LEN, FOX_HEADS, HEAD_DIM), jnp.float32),
        'cache_b_logf': jax.nn.log_sigmoid(3.0 + nrm(ks[6], (DEPTH, DEC_BATCH, PAST_LEN, FOX_HEADS), jnp.float32)),
        'meta_tokens': nrm(ks[7], (N_META, D_MODEL), jnp.float32),
        'norm_g': 1.0 + 0.1 * nrm(ks[8], (DEPTH, D_MODEL), jnp.float32),
        'w_in': nrm(ks[9], (DEPTH, D_MODEL, IN_WIDTH), jnp.float32) * D_MODEL ** -0.5,
        'b_f': 2.0 + 2.0 * jax.random.uniform(ks[10], (DEPTH, FOX_HEADS), jnp.float32),
        'w_out': nrm(ks[11], (DEPTH, MIX_WIDTH, D_MODEL), jnp.float32) * MIX_WIDTH ** -0.5,
        'final_g': 1.0 + 0.1 * nrm(ks[12], (D_MODEL,), jnp.float32),
    }


def reference(x_prompt, x_sample, cache_a_k, cache_a_v, cache_b_k, cache_b_v, cache_b_logf,
              meta_tokens, norm_g, w_in, b_f, w_out, final_g):
    B = x_prompt.shape[0]
    meta = jnp.broadcast_to(meta_tokens[None].astype(x_prompt.dtype), (B, N_META, D_MODEL))
    hp = jnp.concatenate([meta, x_prompt], axis=1)
    hs = x_sample
    pak, pav, pbk, pbv, pbl = [], [], [], [], []
    sak, sav, sbk, sbv, sbl = [], [], [], [], []
    for l in range(DEPTH):
        qa, ka, va, ga, qb, kb, vb, gb, lf = project(hp, norm_g[l], w_in[l], b_f[l])
        oa, ob = attend_prompt(qa, ka, va, qb, kb, vb, lf)
        hp = combine(hp, oa, ga, ob, gb, w_out[l])
        pak.append(ka); pav.append(va); pbk.append(kb); pbv.append(vb); pbl.append(lf)
        qa, ka, va, ga, qb, kb, vb, gb, lf = project(hs, norm_g[l], w_in[l], b_f[l])
        oa, ob = attend_sample(qa, ka, va, qb, kb, vb, lf, cache_a_k[l], cache_a_v[l],
                               cache_b_k[l], cache_b_v[l], cache_b_logf[l])
        hs = combine(hs, oa, ga, ob, gb, w_out[l])
        sak.append(ka); sav.append(va); sbk.append(kb); sbv.append(vb); sbl.append(lf)
    y_prompt = rmsnorm(hp, final_g)[:, N_META:]
    y_sample = rmsnorm(hs, final_g)
    return (y_prompt, y_sample,
            jnp.stack(pak), jnp.stack(pav), jnp.stack(pbk), jnp.stack(pbv), jnp.stack(pbl),
            jnp.stack(sak), jnp.stack(sav), jnp.stack(sbk), jnp.stack(sbv), jnp.stack(sbl))
```

```python
import functools

import jax
import jax.numpy as jnp
import numpy as np
from jax import lax
from jax.experimental import pallas as pl
from jax.experimental.pallas import tpu as pltpu

F32 = jnp.float32
BF16 = jnp.bfloat16

D_MODEL = 1024
HEAD_DIM = 64
HEADS = 8
WIDTH = HEADS * HEAD_DIM
N_META = 16
EPS = 1e-6
LANES = 128
KEY_BLOCK = 128
Q_ROWS = 128
PAD_KEYS = KEY_BLOCK - N_META
NEG_BIG = -1e30
VMEM_LIMIT = 48 * 1024 * 1024


def _split3(x):
    hi = x.astype(BF16)
    r1 = x - hi.astype(F32)
    mid = r1.astype(BF16)
    lo = (r1 - mid.astype(F32)).astype(BF16)
    return hi, mid, lo


def _log_sigmoid(x):
    return jnp.minimum(x, 0.0) - jnp.log(1.0 + jnp.exp(-jnp.abs(x)))


def _silu(g):
    return g / (1.0 + jnp.exp(-g))


def _dot_nt(a, b):
    return lax.dot_general(a, b, (((1,), (1,)), ((), ())), preferred_element_type=F32)


def _neg_split_lanes(c, lane):
    hi, mid, lo = _split3(-c)
    zero = jnp.zeros_like(hi)
    return jnp.where(lane < 8, hi, jnp.where(lane < 16, mid, jnp.where(lane < 24, lo, zero)))


def _cache_cumsum_body(x_ref, tri_ref, cs_ref, tot_ref, carry_ref):
    t = pl.program_id(1)

    @pl.when(t == 0)
    def _():
        carry_ref[...] = jnp.zeros_like(carry_ref)

    x = x_ref[0]
    tm = x.shape[0]
    lane = lax.broadcasted_iota(jnp.int32, x.shape, 1)
    hi, mid, lo = _split3(x)
    tri = tri_ref[...]
    c = (jnp.dot(tri, hi, preferred_element_type=F32) + jnp.dot(tri, mid, preferred_element_type=F32)
         + jnp.dot(tri, lo, preferred_element_type=F32)) + carry_ref[0:1, :]
    last = jnp.broadcast_to(c[tm - 1:tm, :], carry_ref.shape)
    carry_ref[...] = last
    cs_ref[0] = _neg_split_lanes(c, lane)
    tot_ref[0] = last


def _cache_cumsum(x3, tri, tm):
    b, p, _ = x3.shape
    return pl.pallas_call(
        _cache_cumsum_body,
        grid=(b, p // tm),
        in_specs=[pl.BlockSpec((1, tm, LANES), lambda i, t: (i, t, 0)),
                  pl.BlockSpec((tm, tm), lambda i, t: (0, 0))],
        out_specs=[pl.BlockSpec((1, tm, LANES), lambda i, t: (i, t, 0)),
                   pl.BlockSpec((1, 8, LANES), lambda i, t: (i, 0, 0))],
        out_shape=[jax.ShapeDtypeStruct((b, p, LANES), BF16),
                   jax.ShapeDtypeStruct((b, 8, LANES), F32)],
        scratch_shapes=[pltpu.VMEM((8, LANES), F32)],
        compiler_params=pltpu.CompilerParams(dimension_semantics=("arbitrary", "arbitrary"),
                                             vmem_limit_bytes=VMEM_LIMIT),
        name="cache_cumsum",
    )(x3, tri)


def _proj_body(*refs, has_c0, use_carry):
    if has_c0:
        x_ref, g_ref, wm_ref, wf_ref, bf_ref, tri_ref, c0_ref = refs[:7]
        rest = refs[7:]
    else:
        x_ref, g_ref, wm_ref, wf_ref, bf_ref, tri_ref = refs[:6]
        c0_ref = None
        rest = refs[6:]
    slab_refs = rest[:8]
    lf_ref, cs_ref, carry_ref = rest[8:]
    t = pl.program_id(1)

    x = x_ref[0]
    ms = jnp.mean(x * x, axis=-1, keepdims=True)
    xb = ((x * lax.rsqrt(ms + EPS)) * g_ref[...]).astype(BF16)
    for c, o_ref in enumerate(slab_refs):
        u = jnp.dot(xb, wm_ref[:, c * WIDTH:(c + 1) * WIDTH], preferred_element_type=F32)
        if c % 4 == 0:
            u = u * (1.0 / 8.0)
        o_ref[0] = u.astype(o_ref.dtype)

    fl = jnp.dot(xb, wf_ref[...], preferred_element_type=F32) + bf_ref[...]
    logf = _log_sigmoid(fl)
    lf_ref[0] = logf[:, 0:HEADS]
    tm = x.shape[0]
    lane = lax.broadcasted_iota(jnp.int32, (tm, LANES), 1)
    hi, mid, lo = _split3(jnp.where(lane < 3 * HEADS, logf, 0.0))
    tri = tri_ref[...]
    c = (jnp.dot(tri, hi, preferred_element_type=F32) + jnp.dot(tri, mid, preferred_element_type=F32)
         + jnp.dot(tri, lo, preferred_element_type=F32))
    if has_c0:
        c = c + c0_ref[0]
    if use_carry:
        @pl.when(t == 0)
        def _():
            carry_ref[...] = jnp.zeros_like(carry_ref)

        c = c + carry_ref[0:1, :]
        carry_ref[...] = jnp.broadcast_to(c[tm - 1:tm, :], carry_ref.shape)
    cs_ref[0] = _neg_split_lanes(c, lane)


def _project(x, rows_out, g, wm, wf, bf, tri, c0, tm, use_carry):
    b, rows_in, _ = x.shape
    nt = rows_in // tm
    has_c0 = c0 is not None
    row_spec = lambda w: pl.BlockSpec((1, tm, w), lambda i, t: (i, t, 0))
    const = lambda s: pl.BlockSpec(s, lambda i, t: (0,) * len(s))
    in_specs = [row_spec(D_MODEL), const((1, D_MODEL)), const(wm.shape), const(wf.shape),
                const((1, LANES)), const((tm, tm))]
    args = [x, g, wm, wf, bf, tri]
    if has_c0:
        in_specs.append(row_spec(LANES))
        args.append(c0)
    slab_dtypes = [BF16, F32, F32, F32, BF16, F32, F32, F32]
    out_shape = [jax.ShapeDtypeStruct((b, rows_out, WIDTH), dt) for dt in slab_dtypes]
    out_shape += [jax.ShapeDtypeStruct((b, rows_out, HEADS), F32),
                  jax.ShapeDtypeStruct((b, rows_out, LANES), BF16)]
    out_specs = [row_spec(WIDTH)] * 8 + [row_spec(HEADS), row_spec(LANES)]
    return pl.pallas_call(
        functools.partial(_proj_body, has_c0=has_c0, use_carry=use_carry),
        grid=(b, nt),
        in_specs=in_specs,
        out_specs=out_specs,
        out_shape=out_shape,
        scratch_shapes=[pltpu.VMEM((8, LANES), F32)],
        compiler_params=pltpu.CompilerParams(dimension_semantics=("arbitrary", "arbitrary"),
                                             vmem_limit_bytes=VMEM_LIMIT),
        name="project",
    )(*args)


def _sb_update(qv, kblk, vblk, tt, carry, acc, valid):
    z = _dot_nt(qv, kblk)
    lk = jnp.minimum(-z, 0.0) - jnp.log(1.0 + jnp.exp(-jnp.abs(z)))
    if valid is not None:
        lk = jnp.where(valid, lk, 0.0)
    hi = lk.astype(BF16)
    lo = (lk - hi.astype(F32)).astype(BF16)
    cs = jnp.dot(jnp.concatenate([hi, lo], axis=1), tt, preferred_element_type=F32)
    w = jnp.exp(z + cs[:, :KEY_BLOCK] + carry)
    if valid is not None:
        w = jnp.where(valid, w, 0.0)
    acc = acc + jnp.dot(w.astype(BF16), vblk, preferred_element_type=F32)
    carry = carry + cs[:, KEY_BLOCK:]
    return carry, acc


def _fox_update(qv, qone, kblk, csblk, vblk, m, l, acc, valid):
    s = _dot_nt(qv, kblk) + _dot_nt(qone, csblk)
    if valid is not None:
        s = jnp.where(valid, s, NEG_BIG)
    m_new = jnp.maximum(m, jnp.max(s, axis=-1, keepdims=True))
    alpha = jnp.exp(m - m_new)
    p = jnp.exp(s - m_new)
    l = alpha * l + jnp.sum(p, axis=-1, keepdims=True)
    acc = alpha * acc + jnp.dot(p.astype(BF16), vblk, preferred_element_type=F32)
    return m_new, l, acc


def _one_hot_bias_rows(head, shape):
    col = lax.broadcasted_iota(jnp.int32, shape, 1)
    hit = (col == head) | (col == head + HEADS) | (col == head + 2 * HEADS)
    return jnp.where(hit, 1.0, 0.0).astype(BF16)


def _fill_padded(dst, src_ref):
    dst[0:PAD_KEYS, :] = jnp.zeros((PAD_KEYS, dst.shape[1]), BF16)
    dst[PAD_KEYS:, :] = src_ref[0].astype(BF16)


def _prompt_masks():
    rows = 2 * Q_ROWS
    row = lax.broadcasted_iota(jnp.int32, (rows, KEY_BLOCK), 0)
    col = lax.broadcasted_iota(jnp.int32, (rows, KEY_BLOCK), 1)
    qidx = row & (Q_ROWS - 1)
    head_mask = (row < Q_ROWS) == (col < HEAD_DIM)
    first_head_lanes = lax.broadcasted_iota(jnp.int32, (Q_ROWS, LANES), 1) < HEAD_DIM
    return row, col, qidx, head_mask, first_head_lanes


def _sb_prompt_body(q_ref, k_ref, v_ref, g_ref, tt_ref, o_ref, k16, v16, *, n_qblocks):
    _fill_padded(k16, k_ref)
    _fill_padded(v16, v_ref)
    row, col, qidx, head_mask, first_head_lanes = _prompt_masks()
    diag_valid = col < qidx
    meta_valid = col >= PAD_KEYS
    tt = tt_ref[...]
    rows = 2 * Q_ROWS

    def q_block(qi, _):
        qs = pl.multiple_of(N_META + qi * Q_ROWS, 16)
        q = q_ref[0, pl.ds(qs, Q_ROWS), :]
        qv = jnp.where(head_mask, jnp.concatenate([q, q], axis=0), jnp.zeros((), BF16))

        def tile(kb, carry, acc, valid):
            ks = pl.multiple_of(kb * KEY_BLOCK, KEY_BLOCK)
            return _sb_update(qv, k16[pl.ds(ks, KEY_BLOCK), :], v16[pl.ds(ks, KEY_BLOCK), :],
                              tt, carry, acc, valid)

        zero = jnp.zeros((rows, KEY_BLOCK), F32)
        carry, acc = tile(qi + 1, zero, zero, diag_valid)
        carry, acc = lax.fori_loop(0, qi, lambda j, ca: tile(qi - j, ca[0], ca[1], None), (carry, acc))
        carry, acc = tile(0, carry, acc, meta_valid)
        out = jnp.where(first_head_lanes, acc[:Q_ROWS], acc[Q_ROWS:])
        g = g_ref[0, pl.ds(qs, Q_ROWS), :]
        os_ = pl.multiple_of(qi * Q_ROWS, Q_ROWS)
        o_ref[0, pl.ds(os_, Q_ROWS), :] = (out * _silu(g)).astype(o_ref.dtype)
        return 0

    lax.fori_loop(0, n_qblocks, q_block, 0)


def _fox_prompt_body(q_ref, k_ref, v_ref, cs_ref, g_ref, o_ref, k16, v16, c16, *, n_qblocks):
    _fill_padded(k16, k_ref)
    _fill_padded(v16, v_ref)
    _fill_padded(c16, cs_ref)
    row, col, qidx, head_mask, first_head_lanes = _prompt_masks()
    diag_valid = col <= qidx
    meta_valid = col >= PAD_KEYS
    rows = 2 * Q_ROWS
    head = 2 * pl.program_id(1) + jnp.where(row < Q_ROWS, 0, 1)
    qone = _one_hot_bias_rows(head, (rows, LANES))

    def q_block(qi, _):
        qs = pl.multiple_of(N_META + qi * Q_ROWS, 16)
        q = q_ref[0, pl.ds(qs, Q_ROWS), :]
        qv = jnp.where(head_mask, jnp.concatenate([q, q], axis=0), jnp.zeros((), BF16))

        def tile(kb, m, l, acc, valid):
            ks = pl.multiple_of(kb * KEY_BLOCK, KEY_BLOCK)
            sl = pl.ds(ks, KEY_BLOCK)
            return _fox_update(qv, qone, k16[sl, :], c16[sl, :], v16[sl, :], m, l, acc, valid)

        m0 = jnp.full((rows, 1), NEG_BIG, F32)
        l0 = jnp.zeros((rows, 1), F32)
        a0 = jnp.zeros((rows, KEY_BLOCK), F32)
        m, l, acc = tile(qi + 1, m0, l0, a0, diag_valid)
        m, l, acc = lax.fori_loop(0, qi, lambda j, c: tile(qi - j, c[0], c[1], c[2], None), (m, l, acc))
        m, l, acc = tile(0, m, l, acc, meta_valid)
        o = acc / l
        out = jnp.where(first_head_lanes, o[:Q_ROWS], o[Q_ROWS:])
        g = g_ref[0, pl.ds(qs, Q_ROWS), :]
        os_ = pl.multiple_of(qi * Q_ROWS, Q_ROWS)
        o_ref[0, pl.ds(os_, Q_ROWS), :] = (out * _silu(g)).astype(o_ref.dtype)
        return 0

    lax.fori_loop(0, n_qblocks, q_block, 0)


def _prompt_attention(kind, q, k, v, g, extra, seq):
    b, length, _ = q.shape
    n_pairs = WIDTH // LANES
    padded = PAD_KEYS + length
    pair = pl.BlockSpec((1, length, LANES), lambda i, p: (i, 0, p))
    out_spec = pl.BlockSpec((1, seq, LANES), lambda i, p: (i, 0, p))
    scratch = [pltpu.VMEM((padded, LANES), BF16), pltpu.VMEM((padded, LANES), BF16)]
    if kind == "sb":
        body = functools.partial(_sb_prompt_body, n_qblocks=seq // Q_ROWS)
        in_specs = [pair, pair, pair, pair, pl.BlockSpec(extra.shape, lambda i, p: (0, 0))]
        args = (q, k, v, g, extra)
    else:
        body = functools.partial(_fox_prompt_body, n_qblocks=seq // Q_ROWS)
        in_specs = [pair, pair, pair, pl.BlockSpec((1, length, LANES), lambda i, p: (i, 0, 0)), pair]
        args = (q, k, v, extra, g)
        scratch.append(pltpu.VMEM((padded, LANES), BF16))
    return pl.pallas_call(
        body,
        grid=(b, n_pairs),
        in_specs=in_specs,
        out_specs=out_spec,
        out_shape=jax.ShapeDtypeStruct((b, seq, WIDTH), BF16),
        scratch_shapes=scratch,
        compiler_params=pltpu.CompilerParams(dimension_semantics=("arbitrary", "arbitrary"),
                                             vmem_limit_bytes=VMEM_LIMIT),
        name=kind + "_prompt",
    )(*args)


def _sample_setup(q_ref, qv_s, dec):
    q = q_ref[0]
    lane_head = lax.broadcasted_iota(jnp.int32, q.shape, 1) >> 6
    for h in range(HEADS):
        qv_s[h * dec:(h + 1) * dec, :] = jnp.where(lane_head == h, q, jnp.zeros((), BF16))
    return lane_head


def _pad_rows_bf16(x, rows):
    x = x.astype(BF16)
    return jnp.concatenate([x, jnp.zeros((rows - x.shape[0], x.shape[1]), BF16)], axis=0)


def _gather_heads(acc, lane_head, dec):
    out = jnp.zeros((dec, WIDTH), F32)
    for h in range(HEADS):
        out = jnp.where(lane_head == h, acc[h * dec:(h + 1) * dec, :], out)
    return out


def _sb_sample_body(q_ref, kn_ref, vn_ref, g_ref, ck_ref, cv_ref, tt_ref, o_ref,
                    qv_s, carry_s, acc_s, *, dec, n_sub):
    j = pl.program_id(1)
    rows = HEADS * dec
    tt = tt_ref[...]

    @pl.when(j == 0)
    def _():
        _sample_setup(q_ref, qv_s, dec)
        row = lax.broadcasted_iota(jnp.int32, (rows, KEY_BLOCK), 0)
        col = lax.broadcasted_iota(jnp.int32, (rows, KEY_BLOCK), 1)
        valid = col < (row & (dec - 1))
        carry, acc = _sb_update(qv_s[...], _pad_rows_bf16(kn_ref[0], KEY_BLOCK),
                                _pad_rows_bf16(vn_ref[0], KEY_BLOCK), tt,
                                jnp.zeros((rows, KEY_BLOCK), F32), jnp.zeros((rows, WIDTH), F32), valid)
        carry_s[...] = carry
        acc_s[...] = acc

    qv = qv_s[...]
    carry = carry_s[...]
    acc = acc_s[...]
    for sub in reversed(range(n_sub)):
        sl = slice(sub * KEY_BLOCK, (sub + 1) * KEY_BLOCK)
        carry, acc = _sb_update(qv, ck_ref[0, sl, :].astype(BF16), cv_ref[0, sl, :].astype(BF16),
                                tt, carry, acc, None)
    carry_s[...] = carry
    acc_s[...] = acc

    @pl.when(j == pl.num_programs(1) - 1)
    def _():
        lane_head = lax.broadcasted_iota(jnp.int32, (dec, WIDTH), 1) >> 6
        out = _gather_heads(acc, lane_head, dec)
        o_ref[0] = (out * _silu(g_ref[0])).astype(o_ref.dtype)


def _fox_sample_body(q_ref, kn_ref, vn_ref, csn_ref, g_ref, ck_ref, cv_ref, csc_ref, o_ref,
                     qv_s, m_s, l_s, acc_s, *, dec, n_sub):
    j = pl.program_id(1)
    rows = HEADS * dec
    row = lax.broadcasted_iota(jnp.int32, (rows, LANES), 0)
    qone = _one_hot_bias_rows(row >> (dec.bit_length() - 1), (rows, LANES))

    @pl.when(j == 0)
    def _():
        _sample_setup(q_ref, qv_s, dec)
        col = lax.broadcasted_iota(jnp.int32, (rows, KEY_BLOCK), 1)
        valid = col <= (row & (dec - 1))
        m, l, acc = _fox_update(qv_s[...], qone, _pad_rows_bf16(kn_ref[0], KEY_BLOCK),
                                _pad_rows_bf16(csn_ref[0], KEY_BLOCK),
                                _pad_rows_bf16(vn_ref[0], KEY_BLOCK),
                                jnp.full((rows, 1), NEG_BIG, F32), jnp.zeros((rows, 1), F32),
                                jnp.zeros((rows, WIDTH), F32), valid)
        m_s[...] = m
        l_s[...] = l
        acc_s[...] = acc

    qv = qv_s[...]
    m = m_s[...]
    l = l_s[...]
    acc = acc_s[...]
    for sub in reversed(range(n_sub)):
        sl = slice(sub * KEY_BLOCK, (sub + 1) * KEY_BLOCK)
        m, l, acc = _fox_update(qv, qone, ck_ref[0, sl, :].astype(BF16), csc_ref[0, sl, :],
                                cv_ref[0, sl, :].astype(BF16), m, l, acc, None)
    m_s[...] = m
    l_s[...] = l
    acc_s[...] = acc

    @pl.when(j == pl.num_programs(1) - 1)
    def _():
        lane_head = lax.broadcasted_iota(jnp.int32, (dec, WIDTH), 1) >> 6
        out = _gather_heads(acc / l, lane_head, dec)
        o_ref[0] = (out * _silu(g_ref[0])).astype(o_ref.dtype)


def _sample_attention(kind, q, kn, vn, g, cache_k, cache_v, extra, tkb):
    b, dec, _ = q.shape
    past = cache_k.shape[1]
    nkb = past // tkb
    rows = HEADS * dec
    new = lambda w: pl.BlockSpec((1, dec, w), lambda i, j: (i, 0, 0))
    old = lambda w: pl.BlockSpec((1, tkb, w), lambda i, j: (i, nkb - 1 - j, 0))
    if kind == "sb":
        body = functools.partial(_sb_sample_body, dec=dec, n_sub=tkb // KEY_BLOCK)
        in_specs = [new(WIDTH), new(WIDTH), new(WIDTH), new(WIDTH), old(WIDTH), old(WIDTH),
                    pl.BlockSpec(extra.shape, lambda i, j: (0, 0))]
        args = (q, kn, vn, g, cache_k, cache_v, extra)
        scratch = [pltpu.VMEM((rows, WIDTH), BF16), pltpu.VMEM((rows, KEY_BLOCK), F32),
                   pltpu.VMEM((rows, WIDTH), F32)]
    else:
        csn, csc = extra
        body = functools.partial(_fox_sample_body, dec=dec, n_sub=tkb // KEY_BLOCK)
        in_specs = [new(WIDTH), new(WIDTH), new(WIDTH), new(LANES), new(WIDTH), old(WIDTH), old(WIDTH),
                    old(LANES)]
        args = (q, kn, vn, csn, g, cache_k, cache_v, csc)
        scratch = [pltpu.VMEM((rows, WIDTH), BF16), pltpu.VMEM((rows, 1), F32),
                   pltpu.VMEM((rows, 1), F32), pltpu.VMEM((rows, WIDTH), F32)]
    return pl.pallas_call(
        body,
        grid=(b, nkb),
        in_specs=in_specs,
        out_specs=new(WIDTH),
        out_shape=jax.ShapeDtypeStruct((b, dec, WIDTH), BF16),
        scratch_shapes=scratch,
        compiler_params=pltpu.CompilerParams(dimension_semantics=("arbitrary", "arbitrary"),
                                             vmem_limit_bytes=VMEM_LIMIT),
        name=kind + "_sample",
    )(*args)


def _out_body(ma_ref, mb_ref, x_ref, wa_ref, wb_ref, g_ref, y_ref):
    h = (x_ref[...] + jnp.dot(ma_ref[...], wa_ref[...], preferred_element_type=F32)
         + jnp.dot(mb_ref[...], wb_ref[...], preferred_element_type=F32))
    ms = jnp.mean(h * h, axis=-1, keepdims=True)
    y_ref[...] = (h * lax.rsqrt(ms + EPS)) * g_ref[...]


def _output(ma, mb, x, wa, wb, g, tm):
    r = x.shape[0]
    row = lambda w: pl.BlockSpec((tm, w), lambda i: (i, 0))
    const = lambda s: pl.BlockSpec(s, lambda i: (0, 0))
    return pl.pallas_call(
        _out_body,
        grid=(r // tm,),
        in_specs=[row(WIDTH), row(WIDTH), row(D_MODEL), const(wa.shape), const(wb.shape),
                  const((1, D_MODEL))],
        out_specs=row(D_MODEL),
        out_shape=jax.ShapeDtypeStruct((r, D_MODEL), F32),
        compiler_params=pltpu.CompilerParams(dimension_semantics=("arbitrary",),
                                             vmem_limit_bytes=VMEM_LIMIT),
        name="output",
    )(ma, mb, x, wa, wb, g)


def _lower_tri(n, group=None):
    i = np.arange(n)
    t = i[:, None] >= i[None, :]
    if group is not None:
        t &= (i[:, None] // group) == (i[None, :] // group)
    return jnp.asarray(t.astype(np.float32), dtype=BF16)


def _suffix_sum_matrix():
    j = np.arange(2 * KEY_BLOCK) % KEY_BLOCK
    s = np.arange(KEY_BLOCK)
    t = (j[:, None] >= s[None, :]).astype(np.float32)
    return jnp.asarray(np.concatenate([t, np.ones_like(t)], axis=1), dtype=BF16)


def kernel(x_prompt, x_sample, cache_a_k, cache_a_v, cache_b_k, cache_b_v, cache_b_logf,
           meta_tokens, norm_g, w_in, b_f, w_out, final_g):
    depth = norm_g.shape[0]
    assert depth == 1, "single-layer trunk only"
    batch, seq, _ = x_prompt.shape
    dec_batch, dec_seq, _ = x_sample.shape
    past = cache_a_k.shape[2]
    length = N_META + seq
    tm = 256

    w = w_in[0]
    wm = w[:, :8 * WIDTH].astype(BF16)
    wf = jnp.pad(jnp.tile(w[:, 8 * WIDTH:], (1, 3)), ((0, 0), (0, LANES - 3 * HEADS))).astype(BF16)
    bf = jnp.pad(jnp.tile(b_f[0], 3), (0, LANES - 3 * HEADS)).reshape(1, LANES).astype(F32)
    g_in = norm_g[0].reshape(1, D_MODEL)
    g_out = final_g.reshape(1, D_MODEL)
    wa = w_out[0, :WIDTH].astype(BF16)
    wb = w_out[0, WIDTH:].astype(BF16)
    tt = _suffix_sum_matrix()
    tri = _lower_tri(tm)

    rows_in = -(-length // tm) * tm
    meta = jnp.broadcast_to(meta_tokens[None].astype(x_prompt.dtype), (batch, N_META, D_MODEL))
    hp = jnp.concatenate([meta, x_prompt, jnp.zeros((batch, rows_in - length, D_MODEL), x_prompt.dtype)],
                         axis=1)
    (qa, ka, va, ga, qb, kb, vb, gb, lf, cs) = _project(hp, length, g_in, wm, wf, bf, tri, None, tm, True)
    mixed_a = _prompt_attention("sb", qa, ka, va, ga, tt, seq)
    mixed_b = _prompt_attention("fox", qb, kb, vb, gb, cs, seq)
    y_prompt = _output(mixed_a.reshape(batch * seq, WIDTH), mixed_b.reshape(batch * seq, WIDTH),
                       x_prompt.reshape(batch * seq, D_MODEL), wa, wb, g_out, tm).reshape(batch, seq, D_MODEL)

    clf = cache_b_logf[0]
    clf3 = jnp.pad(jnp.tile(clf, (1, 1, 3)), ((0, 0), (0, 0), (0, LANES - 3 * HEADS)))
    csc, tot = _cache_cumsum(clf3, tri, tm)
    s_rows = dec_batch * dec_seq
    c0 = jnp.repeat(tot[:, 0, :], dec_seq, axis=0).reshape(1, s_rows, LANES)
    tri_s = _lower_tri(tm, group=dec_seq)
    (sqa, ska, sva, sga, sqb, skb, svb, sgb, slf, scs) = _project(
        x_sample.reshape(1, s_rows, D_MODEL), s_rows, g_in, wm, wf, bf, tri_s, c0, tm, False)
    per_b = lambda a: a.reshape(dec_batch, dec_seq, a.shape[-1])
    flat = lambda c: c[0].reshape(dec_batch, past, WIDTH)
    tkb = 512
    smix_a = _sample_attention("sb", per_b(sqa), per_b(ska), per_b(sva), per_b(sga),
                               flat(cache_a_k), flat(cache_a_v), tt, tkb)
    smix_b = _sample_attention("fox", per_b(sqb), per_b(skb), per_b(svb), per_b(sgb),
                               flat(cache_b_k), flat(cache_b_v), (per_b(scs), csc), tkb)
    y_sample = _output(smix_a.reshape(s_rows, WIDTH), smix_b.reshape(s_rows, WIDTH),
                       x_sample.reshape(s_rows, D_MODEL), wa, wb, g_out, tm).reshape(dec_batch, dec_seq, D_MODEL)

    heads = lambda a, bsz, t: a.reshape(1, bsz, t, HEADS, HEAD_DIM)
    return (y_prompt, y_sample,
            heads(ka, batch, length), heads(va, batch, length),
            heads(kb, batch, length), heads(vb, batch, length),
            lf.reshape(1, batch, length, HEADS),
            heads(ska, dec_batch, dec_seq), heads(sva, dec_batch, dec_seq),
            heads(skb, dec_batch, dec_seq), heads(svb, dec_batch, dec_seq),
            slf.reshape(1, dec_batch, dec_seq, HEADS))
```

```python
import functools

import jax
import jax.numpy as jnp
import numpy as np
from jax import lax
from jax.experimental import pallas as pl
from jax.experimental.pallas import tpu as pltpu

F32 = jnp.float32
BF16 = jnp.bfloat16

D_MODEL = 1024
HEAD_DIM = 64
HEADS = 8
WIDTH = HEADS * HEAD_DIM
N_META = 16
EPS = 1e-6
LANES = 128
SUB = 128
CHUNK = 512
Q_ROWS = 128
BIAS_ROWS = 32
PAD_KEYS = SUB - N_META
PROMPT_STREAMS = 2
NEG_BIG = -1e30
VMEM_LIMIT = 48 * 1024 * 1024


def _split3(x):
    hi = x.astype(BF16)
    r1 = x - hi.astype(F32)
    mid = r1.astype(BF16)
    lo = (r1 - mid.astype(F32)).astype(BF16)
    return hi, mid, lo


def _log_sigmoid(x):
    return jnp.minimum(x, 0.0) - jnp.log(1.0 + jnp.exp(-jnp.abs(x)))


def _silu(g):
    return g / (1.0 + jnp.exp(-g))


def _dot(a, b):
    return jnp.dot(a, b, preferred_element_type=F32)


def _dot_nt(a, b):
    return lax.dot_general(a, b, (((1,), (1,)), ((), ())), preferred_element_type=F32)


def _rms_normed(x, g):
    ms = jnp.mean(x * x, axis=-1, keepdims=True)
    return ((x * lax.rsqrt(ms + EPS)) * g).astype(BF16)


def _cache_bias_body(x_ref, w_ref, hi_ref, mid_ref, lo_ref, carry_ref):
    @pl.when(pl.program_id(0) == 0)
    def _():
        carry_ref[...] = jnp.zeros_like(carry_ref)

    x = x_ref[...]
    tb = x.shape[1]
    hi, mid, lo = _split3(x)
    w = w_ref[...]
    r = _dot(hi, w) + _dot(mid, w) + _dot(lo, w)
    carry = carry_ref[...]
    s = r[:, :tb] + jnp.concatenate([carry] * (tb // LANES), axis=1)
    carry_ref[...] = carry + r[:, tb:]
    h, m, l = _split3(s)
    hi_ref[...] = h.astype(F32)
    mid_ref[...] = m.astype(F32)
    lo_ref[...] = l.astype(F32)


def _cache_bias(x, w, tb):
    rows, past = x.shape
    nb = past // tb
    blk = pl.BlockSpec((rows, tb), lambda t: (0, nb - 1 - t))
    return pl.pallas_call(
        _cache_bias_body,
        grid=(nb,),
        in_specs=[blk, pl.BlockSpec(w.shape, lambda t: (0, 0))],
        out_specs=[blk, blk, blk],
        out_shape=[jax.ShapeDtypeStruct((rows, past), F32)] * 3,
        scratch_shapes=[pltpu.VMEM((rows, LANES), F32)],
        compiler_params=pltpu.CompilerParams(dimension_semantics=("arbitrary",),
                                             vmem_limit_bytes=VMEM_LIMIT),
        name="cache_bias",
    )(x, w)


def _proj_prompt_body(x_ref, g_ref, wrow_ref, wt_ref, wft_ref, bft_ref, triu_ref,
                      qa_ref, ga_ref, qb_ref, gb_ref, kat_ref, vat_ref, kbt_ref, vbt_ref,
                      lft_ref, cst_ref, carry_ref):
    t = pl.program_id(1)
    xb = _rms_normed(x_ref[0], g_ref[...])
    tm = xb.shape[0]
    for c, o_ref in enumerate((qa_ref, ga_ref, qb_ref, gb_ref)):
        u = _dot(xb, wrow_ref[:, c * WIDTH:(c + 1) * WIDTH])
        if c % 2 == 0:
            u = u * (1.0 / 8.0)
        o_ref[0] = u.astype(o_ref.dtype)
    for c, o_ref in enumerate((kat_ref, vat_ref, kbt_ref, vbt_ref)):
        o_ref[0] = _dot_nt(wt_ref[c * WIDTH:(c + 1) * WIDTH, :], xb)

    reps = tm // LANES
    flt = _dot_nt(wft_ref[...], xb) + jnp.concatenate([bft_ref[...]] * reps, axis=1)
    logf = _log_sigmoid(flt)
    lft_ref[0] = logf[0:HEADS]
    hi, mid, lo = _split3(logf)
    triu = triu_ref[...]
    c = _dot(hi, triu) + _dot(mid, triu) + _dot(lo, triu)

    @pl.when(t == 0)
    def _():
        carry_ref[...] = jnp.zeros_like(carry_ref)

    c = c + jnp.concatenate([carry_ref[...]] * reps, axis=1)
    carry_ref[...] = jnp.broadcast_to(c[:, tm - 1:tm], carry_ref.shape)
    h, m, l = _split3(-c[0:HEADS])
    cst_ref[0] = jnp.concatenate([h.astype(F32), m.astype(F32), l.astype(F32),
                                  jnp.zeros((HEADS, tm), F32)], axis=0).astype(BF16)


def _project_prompt(hp, length, g, wrow, wt, wft, bft, triu, tm):
    b, rows_in, _ = hp.shape
    nt = rows_in // tm
    const = lambda s: pl.BlockSpec(s, lambda i, t: (0,) * len(s))
    tok = lambda w: pl.BlockSpec((1, tm, w), lambda i, t: (i, t, 0))
    feat = lambda r: pl.BlockSpec((1, r, tm), lambda i, t: (i, 0, t))
    out_shape = ([jax.ShapeDtypeStruct((b, length, WIDTH), dt) for dt in (BF16, F32, BF16, F32)]
                 + [jax.ShapeDtypeStruct((b, WIDTH, length), F32)] * 4
                 + [jax.ShapeDtypeStruct((b, HEADS, length), F32),
                    jax.ShapeDtypeStruct((b, BIAS_ROWS, length), BF16)])
    return pl.pallas_call(
        _proj_prompt_body,
        grid=(b, nt),
        in_specs=[tok(D_MODEL), const((1, D_MODEL)), const(wrow.shape), const(wt.shape),
                  const(wft.shape), const(bft.shape), const(triu.shape)],
        out_specs=[tok(WIDTH)] * 4 + [feat(WIDTH)] * 4 + [feat(HEADS), feat(BIAS_ROWS)],
        out_shape=out_shape,
        scratch_shapes=[pltpu.VMEM((16, LANES), F32)],
        compiler_params=pltpu.CompilerParams(dimension_semantics=("arbitrary", "arbitrary"),
                                             vmem_limit_bytes=VMEM_LIMIT),
        name="project_prompt",
    )(hp, g, wrow, wt, wft, bft, triu)


def _proj_sample_body(x_ref, g_ref, wm_ref, wf_ref, bf_ref, tri_ref,
                      qa_ref, ka_ref, va_ref, ga_ref, qb_ref, kb_ref, vb_ref, gb_ref, lf_ref, cs_ref):
    xb = _rms_normed(x_ref[...], g_ref[...])
    tm = xb.shape[0]
    for c, o_ref in enumerate((qa_ref, ka_ref, va_ref, ga_ref, qb_ref, kb_ref, vb_ref, gb_ref)):
        u = _dot(xb, wm_ref[:, c * WIDTH:(c + 1) * WIDTH])
        if c % 4 == 0:
            u = u * (1.0 / 8.0)
        o_ref[...] = u.astype(o_ref.dtype)
    logf = _log_sigmoid(_dot(xb, wf_ref[...]) + bf_ref[...])
    lf_ref[...] = logf[:, 0:HEADS]
    lane = lax.broadcasted_iota(jnp.int32, (tm, LANES), 1)
    hi, mid, lo = _split3(jnp.where(lane < 3 * HEADS, logf, 0.0))
    tri = tri_ref[...]
    c = _dot(tri, hi) + _dot(tri, mid) + _dot(tri, lo)
    h, m, l = _split3(-c)
    zero = jnp.zeros_like(h)
    cs_ref[...] = jnp.where(lane < 8, h, jnp.where(lane < 16, m, jnp.where(lane < 24, l, zero)))


def _project_sample(x, g, wm, wf, bf, tri, tm):
    rows = x.shape[0]
    row = lambda w: pl.BlockSpec((tm, w), lambda t: (t, 0))
    const = lambda s: pl.BlockSpec(s, lambda t: (0,) * len(s))
    slab_dtypes = [BF16, F32, F32, F32, BF16, F32, F32, F32]
    out_shape = ([jax.ShapeDtypeStruct((rows, WIDTH), dt) for dt in slab_dtypes]
                 + [jax.ShapeDtypeStruct((rows, HEADS), F32), jax.ShapeDtypeStruct((rows, LANES), BF16)])
    return pl.pallas_call(
        _proj_sample_body,
        grid=(rows // tm,),
        in_specs=[row(D_MODEL), const((1, D_MODEL)), const(wm.shape), const(wf.shape),
                  const((1, LANES)), const((tm, tm))],
        out_specs=[row(WIDTH)] * 8 + [row(HEADS), row(LANES)],
        out_shape=out_shape,
        compiler_params=pltpu.CompilerParams(dimension_semantics=("arbitrary",),
                                             vmem_limit_bytes=VMEM_LIMIT),
        name="project_sample",
    )(x, g, wm, wf, bf, tri)


def _sb_suffix(z, sp, tt, carry, valid):
    hi = sp.astype(BF16)
    lo = (sp - hi.astype(F32)).astype(BF16)
    cs = _dot(jnp.concatenate([hi, lo], axis=1), tt)
    w = jnp.exp(z - cs[:, :SUB] - carry)
    if valid is not None:
        w = jnp.where(valid, w, 0.0)
    return w.astype(BF16), carry + cs[:, SUB:]


def _softplus(z, valid):
    sp = jnp.maximum(z, 0.0) + jnp.log(1.0 + jnp.exp(-jnp.abs(z)))
    if valid is not None:
        sp = jnp.where(valid, sp, 0.0)
    return sp


def _sb_chunk(qv, kt, vt, tt, carry, acc, valid):
    z = _dot(qv, kt)
    sp = _softplus(z, valid)
    ws = []
    for u in reversed(range(z.shape[1] // SUB)):
        sl = slice(u * SUB, (u + 1) * SUB)
        w, carry = _sb_suffix(z[:, sl], sp[:, sl], tt, carry, None if valid is None else valid[:, sl])
        ws.append(w)
    acc = acc + _dot_nt(jnp.concatenate(ws[::-1], axis=1), vt)
    return carry, acc


def _sb_rows(qv, k, v, tt, carry, acc, valid):
    z = _dot_nt(qv, k)
    w, carry = _sb_suffix(z, _softplus(z, valid), tt, carry, valid)
    return carry, acc + _dot(w, v)


def _softmax_step(s, m, l, acc, valid, pv):
    if valid is not None:
        s = jnp.where(valid, s, NEG_BIG)
    m_new = jnp.maximum(m, jnp.max(s, axis=-1, keepdims=True))
    alpha = jnp.exp(m - m_new)
    p = jnp.exp(s - m_new)
    l = alpha * l + jnp.sum(p, axis=-1, keepdims=True)
    acc = alpha * acc + pv(p.astype(BF16))
    return m_new, l, acc


def _one_hot_bias_cols(head, shape):
    col = lax.broadcasted_iota(jnp.int32, shape, 1)
    hit = (col == head) | (col == head + HEADS) | (col == head + 2 * HEADS)
    return jnp.where(hit, 1.0, 0.0).astype(BF16)


def _shifted(src, stage):
    r, length = src.shape
    tail = (length // LANES) * LANES
    stage[0:r, tail:] = jnp.zeros((r, stage.shape[1] - tail), F32)
    stage[0:r, 0:length] = src
    return pltpu.roll(stage[0:r, :], PAD_KEYS, axis=1)


def _prompt_geometry():
    rows = 2 * Q_ROWS
    row = lax.broadcasted_iota(jnp.int32, (rows, CHUNK), 0)
    col = lax.broadcasted_iota(jnp.int32, (rows, CHUNK), 1)
    qidx = row & (Q_ROWS - 1)
    r128 = lax.broadcasted_iota(jnp.int32, (rows, LANES), 0)
    c128 = lax.broadcasted_iota(jnp.int32, (rows, LANES), 1)
    head_mask = (r128 < Q_ROWS) == (c128 < HEAD_DIM)
    first_head_lanes = lax.broadcasted_iota(jnp.int32, (Q_ROWS, LANES), 1) < HEAD_DIM
    return col, qidx, head_mask, first_head_lanes


def _prompt_q_loop(n_qblocks, n_streams, q_ref, g_ref, o_ref, head_mask, first_head_lanes, attend):
    def q_block(qi, _):
        qs = pl.multiple_of(N_META + qi * Q_ROWS, 16)
        q = q_ref[0, pl.ds(qs, Q_ROWS), :]
        qvs = []
        for s in range(n_streams):
            qp = q[:, s * LANES:(s + 1) * LANES]
            qvs.append(jnp.where(head_mask, jnp.concatenate([qp, qp], axis=0), jnp.zeros((), BF16)))
        outs = [jnp.where(first_head_lanes, o[:Q_ROWS], o[Q_ROWS:]) for o in attend(qi, qvs)]
        g = g_ref[0, pl.ds(qs, Q_ROWS), :]
        os_ = pl.multiple_of(qi * Q_ROWS, Q_ROWS)
        o_ref[0, pl.ds(os_, Q_ROWS), :] = (jnp.concatenate(outs, axis=1) * _silu(g)).astype(o_ref.dtype)
        return 0

    lax.fori_loop(0, n_qblocks, q_block, 0)


def _chunk_plan(qi):
    blk = qi + 1
    per = CHUNK // SUB
    return blk // per, (blk % per) * SUB


def _chunk_ref(ref, r0, nrows, c):
    return ref[r0:r0 + nrows, pl.ds(pl.multiple_of(c * CHUNK, CHUNK), CHUNK)]


def _sb_prompt_body(q_ref, kt_ref, vt_ref, g_ref, tt_ref, o_ref, k16, v16, stage, *, n_qblocks, n_streams):
    for s in range(n_streams):
        rs = slice(s * LANES, (s + 1) * LANES)
        k16[rs, :] = _shifted(kt_ref[0, rs, :], stage).astype(BF16)
        v16[rs, :] = _shifted(vt_ref[0, rs, :], stage).astype(BF16)
    col, qidx, head_mask, first_head_lanes = _prompt_geometry()
    tt = tt_ref[...]
    rows = 2 * Q_ROWS

    def attend(qi, qvs):
        last, tile_lane = _chunk_plan(qi)

        def chunk(c, state, valid):
            return tuple(
                _sb_chunk(qvs[s], _chunk_ref(k16, s * LANES, LANES, c), _chunk_ref(v16, s * LANES, LANES, c),
                          tt, carry, acc, valid)
                for s, (carry, acc) in enumerate(state))

        zero = jnp.zeros((rows, LANES), F32)
        state = chunk(last, ((zero, zero),) * n_streams, col - tile_lane < qidx)
        state = lax.fori_loop(0, last, lambda j, st: chunk(last - 1 - j, st, None), state)
        return [acc for _, acc in state]

    _prompt_q_loop(n_qblocks, n_streams, q_ref, g_ref, o_ref, head_mask, first_head_lanes, attend)


def _fox_prompt_body(q_ref, kt_ref, vt_ref, cst_ref, g_ref, o_ref, kc16, v16, stage, *, n_qblocks, n_streams):
    bias = _shifted(cst_ref[0].astype(F32), stage)
    brow = lax.broadcasted_iota(jnp.int32, bias.shape, 0)
    blane = lax.broadcasted_iota(jnp.int32, bias.shape, 1)
    bias = jnp.where((blane < PAD_KEYS) & (brow < HEADS), NEG_BIG, bias).astype(BF16)
    kc_rows = 2 * LANES
    for s in range(n_streams):
        rs = slice(s * LANES, (s + 1) * LANES)
        kc16[s * kc_rows:s * kc_rows + LANES, :] = _shifted(kt_ref[0, rs, :], stage).astype(BF16)
        kc16[s * kc_rows + LANES:s * kc_rows + LANES + BIAS_ROWS, :] = bias
        kc16[s * kc_rows + LANES + BIAS_ROWS:(s + 1) * kc_rows, :] = jnp.zeros(
            (LANES - BIAS_ROWS, kc16.shape[1]), BF16)
        v16[rs, :] = _shifted(vt_ref[0, rs, :], stage).astype(BF16)
    col, qidx, head_mask, first_head_lanes = _prompt_geometry()
    rows = 2 * Q_ROWS
    r128 = lax.broadcasted_iota(jnp.int32, (rows, LANES), 0)
    first_head = 2 * n_streams * pl.program_id(1) + jnp.where(r128 < Q_ROWS, 0, 1)
    qones = [_one_hot_bias_cols(first_head + 2 * s, (rows, LANES)) for s in range(n_streams)]

    def attend(qi, qvs):
        last, tile_lane = _chunk_plan(qi)
        qv2s = [jnp.concatenate([qv, qone], axis=1) for qv, qone in zip(qvs, qones)]

        def chunk(c, state, valid):
            out = []
            for s, (m, l, acc) in enumerate(state):
                vt = _chunk_ref(v16, s * LANES, LANES, c)
                sc = _dot(qv2s[s], _chunk_ref(kc16, s * kc_rows, kc_rows, c))
                out.append(_softmax_step(sc, m, l, acc, valid, lambda p, vt=vt: _dot_nt(p, vt)))
            return tuple(out)

        init = (jnp.full((rows, 1), NEG_BIG, F32), jnp.zeros((rows, 1), F32), jnp.zeros((rows, LANES), F32))
        state = chunk(last, (init,) * n_streams, col - tile_lane <= qidx)
        state = lax.fori_loop(0, last, lambda j, st: chunk(last - 1 - j, st, None), state)
        return [acc / l for _, l, acc in state]

    _prompt_q_loop(n_qblocks, n_streams, q_ref, g_ref, o_ref, head_mask, first_head_lanes, attend)


def _prompt_attention(kind, q, kt, vt, g, extra, seq):
    b, length, _ = q.shape
    n_qblocks = seq // Q_ROWS
    padded = ((PAD_KEYS + length + CHUNK - 1) // CHUNK) * CHUNK
    ns = PROMPT_STREAMS
    wide = ns * LANES
    tok = pl.BlockSpec((1, length, wide), lambda i, p: (i, 0, p))
    feat = pl.BlockSpec((1, wide, length), lambda i, p: (i, p, 0))
    out_spec = pl.BlockSpec((1, seq, wide), lambda i, p: (i, 0, p))
    stage = pltpu.VMEM((LANES, padded), F32)
    if kind == "sb":
        body = functools.partial(_sb_prompt_body, n_qblocks=n_qblocks, n_streams=ns)
        in_specs = [tok, feat, feat, tok, pl.BlockSpec(extra.shape, lambda i, p: (0, 0))]
        args = (q, kt, vt, g, extra)
        scratch = [pltpu.VMEM((wide, padded), BF16), pltpu.VMEM((wide, padded), BF16), stage]
    else:
        body = functools.partial(_fox_prompt_body, n_qblocks=n_qblocks, n_streams=ns)
        in_specs = [tok, feat, feat, pl.BlockSpec((1, BIAS_ROWS, length), lambda i, p: (i, 0, 0)), tok]
        args = (q, kt, vt, extra, g)
        scratch = [pltpu.VMEM((2 * wide, padded), BF16), pltpu.VMEM((wide, padded), BF16), stage]
    return pl.pallas_call(
        body,
        grid=(b, WIDTH // wide),
        in_specs=in_specs,
        out_specs=out_spec,
        out_shape=jax.ShapeDtypeStruct((b, seq, WIDTH), BF16),
        scratch_shapes=scratch,
        compiler_params=pltpu.CompilerParams(dimension_semantics=("arbitrary", "arbitrary"),
                                             vmem_limit_bytes=VMEM_LIMIT),
        name=kind + "_prompt",
    )(*args)


def _sample_setup(q_ref, qv_s, dec):
    q = q_ref[0]
    lane_head = lax.broadcasted_iota(jnp.int32, q.shape, 1) >> 6
    for h in range(HEADS):
        qv_s[h * dec:(h + 1) * dec, :] = jnp.where(lane_head == h, q, jnp.zeros((), BF16))


def _pad_rows_bf16(x, rows):
    x = x.astype(BF16)
    return jnp.concatenate([x, jnp.zeros((rows - x.shape[0], x.shape[1]), BF16)], axis=0)


def _gather_heads(acc, dec):
    lane_head = lax.broadcasted_iota(jnp.int32, (dec, WIDTH), 1) >> 6
    out = jnp.zeros((dec, WIDTH), F32)
    for h in range(HEADS):
        out = jnp.where(lane_head == h, acc[h * dec:(h + 1) * dec, :], out)
    return out


def _sb_sample_body(q_ref, kn_ref, vn_ref, g_ref, ckt_ref, cvt_ref, tt_ref, o_ref,
                    qv_s, carry_s, acc_s, *, dec):
    j = pl.program_id(1)
    rows = HEADS * dec
    tt = tt_ref[...]

    @pl.when(j == 0)
    def _():
        _sample_setup(q_ref, qv_s, dec)
        row = lax.broadcasted_iota(jnp.int32, (rows, SUB), 0)
        col = lax.broadcasted_iota(jnp.int32, (rows, SUB), 1)
        carry, acc = _sb_rows(qv_s[...], _pad_rows_bf16(kn_ref[0], SUB), _pad_rows_bf16(vn_ref[0], SUB),
                              tt, jnp.zeros((rows, SUB), F32), jnp.zeros((rows, WIDTH), F32),
                              col < (row & (dec - 1)))
        carry_s[...] = carry
        acc_s[...] = acc

    carry, acc = _sb_chunk(qv_s[...], ckt_ref[0].astype(BF16), cvt_ref[0].astype(BF16), tt,
                           carry_s[...], acc_s[...], None)
    carry_s[...] = carry
    acc_s[...] = acc

    @pl.when(j == pl.num_programs(1) - 1)
    def _():
        o_ref[0] = (_gather_heads(acc, dec) * _silu(g_ref[0])).astype(o_ref.dtype)


def _fox_sample_body(q_ref, kn_ref, vn_ref, csn_ref, g_ref, ckt_ref, cvt_ref, bh_ref, bm_ref, bl_ref,
                     o_ref, qv_s, m_s, l_s, acc_s, *, dec):
    j = pl.program_id(1)
    rows = HEADS * dec
    head = lax.broadcasted_iota(jnp.int32, (rows, 1), 0) >> (dec.bit_length() - 1)

    @pl.when(j == 0)
    def _():
        _sample_setup(q_ref, qv_s, dec)
        row = lax.broadcasted_iota(jnp.int32, (rows, SUB), 0)
        col = lax.broadcasted_iota(jnp.int32, (rows, SUB), 1)
        vn = _pad_rows_bf16(vn_ref[0], SUB)
        s = (_dot_nt(qv_s[...], _pad_rows_bf16(kn_ref[0], SUB))
             + _dot_nt(_one_hot_bias_cols(head, (rows, LANES)), _pad_rows_bf16(csn_ref[0], SUB)))
        m, l, acc = _softmax_step(s, jnp.full((rows, 1), NEG_BIG, F32), jnp.zeros((rows, 1), F32),
                                  jnp.zeros((rows, WIDTH), F32), col <= (row & (dec - 1)),
                                  lambda p: _dot(p, vn))
        m_s[...] = m
        l_s[...] = l
        acc_s[...] = acc

    tkb = ckt_ref.shape[2]
    bias = jnp.concatenate([bh_ref[0], bm_ref[0], bl_ref[0], jnp.zeros((HEADS, tkb), F32)],
                           axis=0).astype(BF16)
    vt = cvt_ref[0].astype(BF16)
    s = _dot(qv_s[...], ckt_ref[0].astype(BF16)) + _dot(_one_hot_bias_cols(head, (rows, BIAS_ROWS)), bias)
    m, l, acc = _softmax_step(s, m_s[...], l_s[...], acc_s[...], None, lambda p: _dot_nt(p, vt))
    m_s[...] = m
    l_s[...] = l
    acc_s[...] = acc

    @pl.when(j == pl.num_programs(1) - 1)
    def _():
        o_ref[0] = (_gather_heads(acc / l, dec) * _silu(g_ref[0])).astype(o_ref.dtype)


def _sample_attention(kind, q, kn, vn, g, cache_kt, cache_vt, extra, tkb):
    b, dec, _ = q.shape
    past = cache_kt.shape[2]
    nkb = past // tkb
    rows = HEADS * dec
    new = lambda w: pl.BlockSpec((1, dec, w), lambda i, j: (i, 0, 0))
    old = lambda r: pl.BlockSpec((1, r, tkb), lambda i, j: (i, 0, nkb - 1 - j))
    if kind == "sb":
        body = functools.partial(_sb_sample_body, dec=dec)
        in_specs = [new(WIDTH), new(WIDTH), new(WIDTH), new(WIDTH), old(WIDTH), old(WIDTH),
                    pl.BlockSpec(extra.shape, lambda i, j: (0, 0))]
        args = (q, kn, vn, g, cache_kt, cache_vt, extra)
        scratch = [pltpu.VMEM((rows, WIDTH), BF16), pltpu.VMEM((rows, SUB), F32),
                   pltpu.VMEM((rows, WIDTH), F32)]
    else:
        csn, bh, bm, bl = extra
        body = functools.partial(_fox_sample_body, dec=dec)
        in_specs = [new(WIDTH), new(WIDTH), new(WIDTH), new(LANES), new(WIDTH), old(WIDTH), old(WIDTH),
                    old(HEADS), old(HEADS), old(HEADS)]
        args = (q, kn, vn, csn, g, cache_kt, cache_vt, bh, bm, bl)
        scratch = [pltpu.VMEM((rows, WIDTH), BF16), pltpu.VMEM((rows, 1), F32),
                   pltpu.VMEM((rows, 1), F32), pltpu.VMEM((rows, WIDTH), F32)]
    return pl.pallas_call(
        body,
        grid=(b, nkb),
        in_specs=in_specs,
        out_specs=new(WIDTH),
        out_shape=jax.ShapeDtypeStruct((b, dec, WIDTH), BF16),
        scratch_shapes=scratch,
        compiler_params=pltpu.CompilerParams(dimension_semantics=("arbitrary", "arbitrary"),
                                             vmem_limit_bytes=VMEM_LIMIT),
        name=kind + "_sample",
    )(*args)


def _out_body(ma_ref, mb_ref, x_ref, wa_ref, wb_ref, g_ref, y_ref):
    h = x_ref[...] + _dot(ma_ref[...], wa_ref[...]) + _dot(mb_ref[...], wb_ref[...])
    ms = jnp.mean(h * h, axis=-1, keepdims=True)
    y_ref[...] = (h * lax.rsqrt(ms + EPS)) * g_ref[...]


def _output(ma, mb, x, wa, wb, g, tm):
    r = x.shape[0]
    row = lambda w: pl.BlockSpec((tm, w), lambda i: (i, 0))
    const = lambda s: pl.BlockSpec(s, lambda i: (0, 0))
    return pl.pallas_call(
        _out_body,
        grid=(r // tm,),
        in_specs=[row(WIDTH), row(WIDTH), row(D_MODEL), const(wa.shape), const(wb.shape),
                  const((1, D_MODEL))],
        out_specs=row(D_MODEL),
        out_shape=jax.ShapeDtypeStruct((r, D_MODEL), F32),
        compiler_params=pltpu.CompilerParams(dimension_semantics=("arbitrary",),
                                             vmem_limit_bytes=VMEM_LIMIT),
        name="output",
    )(ma, mb, x, wa, wb, g)


def _tri(n, cmp, group=None):
    i = np.arange(n)
    t = cmp(i[:, None], i[None, :])
    if group is not None:
        t &= (i[:, None] // group) == (i[None, :] // group)
    return t.astype(np.float32)


def _suffix_sum_matrix():
    t = _tri(SUB, np.greater_equal)
    half = np.concatenate([t, np.ones_like(t)], axis=1)
    return jnp.asarray(np.concatenate([half, half], axis=0), dtype=BF16)


def kernel(x_prompt, x_sample, cache_a_k, cache_a_v, cache_b_k, cache_b_v, cache_b_logf,
           meta_tokens, norm_g, w_in, b_f, w_out, final_g):
    depth = norm_g.shape[0]
    assert depth == 1, "single-layer trunk only"
    batch, seq, _ = x_prompt.shape
    dec_batch, dec_seq, _ = x_sample.shape
    past = cache_a_k.shape[2]
    length = N_META + seq
    tm = 256

    w = w_in[0]
    wt_all = jnp.transpose(w)
    slab = lambda a, c: a[c * WIDTH:(c + 1) * WIDTH]
    wm = w[:, :8 * WIDTH].astype(BF16)
    wrow = jnp.concatenate([w[:, c * WIDTH:(c + 1) * WIDTH] for c in (0, 3, 4, 7)], axis=1).astype(BF16)
    wt = jnp.concatenate([slab(wt_all, c) for c in (1, 2, 5, 6)], axis=0).astype(BF16)
    wft = jnp.pad(wt_all[8 * WIDTH:], ((0, 16 - HEADS), (0, 0))).astype(BF16)
    bft = jnp.broadcast_to(jnp.pad(b_f[0], (0, 16 - HEADS))[:, None], (16, LANES)).astype(F32)
    wf = jnp.pad(jnp.tile(w[:, 8 * WIDTH:], (1, 3)), ((0, 0), (0, LANES - 3 * HEADS))).astype(BF16)
    bf = jnp.pad(jnp.tile(b_f[0], 3), (0, LANES - 3 * HEADS)).reshape(1, LANES).astype(F32)
    g_in = norm_g[0].reshape(1, D_MODEL)
    g_out = final_g.reshape(1, D_MODEL)
    wa = w_out[0, :WIDTH].astype(BF16)
    wb = w_out[0, WIDTH:].astype(BF16)
    tt = _suffix_sum_matrix()
    triu = jnp.asarray(_tri(tm, np.less_equal), dtype=BF16)
    tri_s = jnp.asarray(_tri(tm, np.greater_equal, group=dec_seq), dtype=BF16)
    suffix_w = jnp.asarray(np.concatenate([_tri(tm, np.greater), np.ones((tm, LANES), np.float32)],
                                          axis=1), dtype=BF16)

    rows_in = -(-length // tm) * tm
    meta = jnp.broadcast_to(meta_tokens[None].astype(x_prompt.dtype), (batch, N_META, D_MODEL))
    hp = jnp.concatenate([meta, x_prompt, jnp.zeros((batch, rows_in - length, D_MODEL), x_prompt.dtype)],
                         axis=1)
    (qa, ga, qb, gb, kat, vat, kbt, vbt, lft, cst) = _project_prompt(
        hp, length, g_in, wrow, wt, wft, bft, triu, tm)
    mixed_a = _prompt_attention("sb", qa, kat, vat, ga, tt, seq)
    mixed_b = _prompt_attention("fox", qb, kbt, vbt, gb, cst, seq)
    y_prompt = _output(mixed_a.reshape(batch * seq, WIDTH), mixed_b.reshape(batch * seq, WIDTH),
                       x_prompt.reshape(batch * seq, D_MODEL), wa, wb, g_out, tm).reshape(batch, seq, D_MODEL)

    s_rows = dec_batch * dec_seq
    (sqa, ska, sva, sga, sqb, skb, svb, sgb, slf, scs) = _project_sample(
        x_sample.reshape(s_rows, D_MODEL), g_in, wm, wf, bf, tri_s, tm)
    clft = jnp.transpose(cache_b_logf[0], (0, 2, 1)).reshape(dec_batch * HEADS, past)
    bias_parts = [p.reshape(dec_batch, HEADS, past) for p in _cache_bias(clft, suffix_w, tm)]
    per_b = lambda a: a.reshape(dec_batch, dec_seq, a.shape[-1])
    feat_major = lambda c: jnp.transpose(c[0], (0, 2, 3, 1)).reshape(dec_batch, WIDTH, past)
    smix_a = _sample_attention("sb", per_b(sqa), per_b(ska), per_b(sva), per_b(sga),
                               feat_major(cache_a_k), feat_major(cache_a_v), tt, CHUNK)
    smix_b = _sample_attention("fox", per_b(sqb), per_b(skb), per_b(svb), per_b(sgb),
                               feat_major(cache_b_k), feat_major(cache_b_v), (per_b(scs), *bias_parts), CHUNK)
    y_sample = _output(smix_a.reshape(s_rows, WIDTH), smix_b.reshape(s_rows, WIDTH),
                       x_sample.reshape(s_rows, D_MODEL), wa, wb, g_out, tm).reshape(dec_batch, dec_seq, D_MODEL)

    def prompt_heads(at):
        return jnp.transpose(at.reshape(batch, HEADS, HEAD_DIM, length), (0, 3, 1, 2))[None]

    sample_heads = lambda a: a.reshape(1, dec_batch, dec_seq, HEADS, HEAD_DIM)
    return (y_prompt, y_sample,
            prompt_heads(kat), prompt_heads(vat), prompt_heads(kbt), prompt_heads(vbt),
            jnp.transpose(lft, (0, 2, 1))[None],
            sample_heads(ska), sample_heads(sva), sample_heads(skb), sample_heads(svb),
            slf.reshape(1, dec_batch, dec_seq, HEADS))
```

```python
import functools

import jax
import jax.numpy as jnp
import numpy as np
from jax import lax
from jax.experimental import pallas as pl
from jax.experimental.pallas import tpu as pltpu

F32 = jnp.float32
BF16 = jnp.bfloat16

D_MODEL = 1024
HEAD_DIM = 64
HEADS = 8
WIDTH = HEADS * HEAD_DIM
N_META = 16
EPS = 1e-6
LANES = 128
SUB = 128
CHUNK = 512
FOX_CHUNK = 1024
Q_ROWS = 128
LOG2_E = 1.4426950408889634
SB_DEAD_CARRY = 104.0
BIAS_ROWS = 32
PAD_KEYS = SUB - N_META
PROMPT_STREAMS = 2
NEG_BIG = -1e30
VMEM_LIMIT = 48 * 1024 * 1024


def _split3(x):
    hi = x.astype(BF16)
    r1 = x - hi.astype(F32)
    mid = r1.astype(BF16)
    lo = (r1 - mid.astype(F32)).astype(BF16)
    return hi, mid, lo


def _log_sigmoid(x):
    return jnp.minimum(x, 0.0) - jnp.log(1.0 + jnp.exp(-jnp.abs(x)))


def _silu(g):
    return g / (1.0 + jnp.exp(-g))


def _dot(a, b):
    return jnp.dot(a, b, preferred_element_type=F32)


def _dot_nt(a, b):
    return lax.dot_general(a, b, (((1,), (1,)), ((), ())), preferred_element_type=F32)


def _rms_normed(x, g):
    ms = jnp.mean(x * x, axis=-1, keepdims=True)
    return ((x * lax.rsqrt(ms + EPS)) * g).astype(BF16)


def _cache_bias_body(x_ref, w_ref, hi_ref, mid_ref, lo_ref, carry_ref):
    @pl.when(pl.program_id(0) == 0)
    def _():
        carry_ref[...] = jnp.zeros_like(carry_ref)

    x = x_ref[...]
    tb = x.shape[1]
    hi, mid, lo = _split3(x)
    w = w_ref[...]
    r = _dot(hi, w) + _dot(mid, w) + _dot(lo, w)
    carry = carry_ref[...]
    s = r[:, :tb] + jnp.concatenate([carry] * (tb // LANES), axis=1)
    carry_ref[...] = carry + r[:, tb:]
    h, m, l = _split3(s)
    hi_ref[...] = h.astype(F32)
    mid_ref[...] = m.astype(F32)
    lo_ref[...] = l.astype(F32)


def _cache_bias(x, w, tb):
    rows, past = x.shape
    nb = past // tb
    blk = pl.BlockSpec((rows, tb), lambda t: (0, nb - 1 - t))
    return pl.pallas_call(
        _cache_bias_body,
        grid=(nb,),
        in_specs=[blk, pl.BlockSpec(w.shape, lambda t: (0, 0))],
        out_specs=[blk, blk, blk],
        out_shape=[jax.ShapeDtypeStruct((rows, past), F32)] * 3,
        scratch_shapes=[pltpu.VMEM((rows, LANES), F32)],
        compiler_params=pltpu.CompilerParams(dimension_semantics=("arbitrary",),
                                             vmem_limit_bytes=VMEM_LIMIT),
        name="cache_bias",
    )(x, w)


def _proj_prompt_body(x_ref, g_ref, wrow_ref, wt_ref, wft_ref, bft_ref, triu_ref,
                      qa_ref, ga_ref, qb_ref, gb_ref, kat_ref, vat_ref, kbt_ref, vbt_ref,
                      lft_ref, cst_ref, carry_ref):
    t = pl.program_id(1)
    xb = _rms_normed(x_ref[0], g_ref[...])
    tm = xb.shape[0]
    for c, o_ref in enumerate((qa_ref, ga_ref, qb_ref, gb_ref)):
        u = _dot(xb, wrow_ref[:, c * WIDTH:(c + 1) * WIDTH])
        if c % 2 == 0:
            u = u * (1.0 / 8.0)
        o_ref[0] = u.astype(o_ref.dtype)
    for c, o_ref in enumerate((kat_ref, vat_ref, kbt_ref, vbt_ref)):
        o_ref[0] = _dot_nt(wt_ref[c * WIDTH:(c + 1) * WIDTH, :], xb)

    reps = tm // LANES
    flt = _dot_nt(wft_ref[...], xb) + jnp.concatenate([bft_ref[...]] * reps, axis=1)
    logf = _log_sigmoid(flt)
    lft_ref[0] = logf[0:HEADS]
    hi, mid, lo = _split3(logf)
    triu = triu_ref[...]
    c = _dot(hi, triu) + _dot(mid, triu) + _dot(lo, triu)

    @pl.when(t == 0)
    def _():
        carry_ref[...] = jnp.zeros_like(carry_ref)

    c = c + jnp.concatenate([carry_ref[...]] * reps, axis=1)
    carry_ref[...] = jnp.broadcast_to(c[:, tm - 1:tm], carry_ref.shape)
    h, m, l = _split3(-c[0:HEADS])
    cst_ref[0] = jnp.concatenate([h.astype(F32), m.astype(F32), l.astype(F32),
                                  jnp.zeros((HEADS, tm), F32)], axis=0).astype(BF16)


def _project_prompt(hp, length, g, wrow, wt, wft, bft, triu, tm):
    b, rows_in, _ = hp.shape
    nt = rows_in // tm
    const = lambda s: pl.BlockSpec(s, lambda i, t: (0,) * len(s))
    tok = lambda w: pl.BlockSpec((1, tm, w), lambda i, t: (i, t, 0))
    feat = lambda r: pl.BlockSpec((1, r, tm), lambda i, t: (i, 0, t))
    out_shape = ([jax.ShapeDtypeStruct((b, length, WIDTH), dt) for dt in (BF16, F32, BF16, F32)]
                 + [jax.ShapeDtypeStruct((b, WIDTH, length), F32)] * 4
                 + [jax.ShapeDtypeStruct((b, HEADS, length), F32),
                    jax.ShapeDtypeStruct((b, BIAS_ROWS, length), BF16)])
    return pl.pallas_call(
        _proj_prompt_body,
        grid=(b, nt),
        in_specs=[tok(D_MODEL), const((1, D_MODEL)), const(wrow.shape), const(wt.shape),
                  const(wft.shape), const(bft.shape), const(triu.shape)],
        out_specs=[tok(WIDTH)] * 4 + [feat(WIDTH)] * 4 + [feat(HEADS), feat(BIAS_ROWS)],
        out_shape=out_shape,
        scratch_shapes=[pltpu.VMEM((16, LANES), F32)],
        compiler_params=pltpu.CompilerParams(dimension_semantics=("arbitrary", "arbitrary"),
                                             vmem_limit_bytes=VMEM_LIMIT),
        name="project_prompt",
    )(hp, g, wrow, wt, wft, bft, triu)


def _proj_sample_body(x_ref, g_ref, wm_ref, wf_ref, bf_ref, tri_ref,
                      qa_ref, ka_ref, va_ref, ga_ref, qb_ref, kb_ref, vb_ref, gb_ref, lf_ref, cs_ref):
    xb = _rms_normed(x_ref[...], g_ref[...])
    tm = xb.shape[0]
    for c, o_ref in enumerate((qa_ref, ka_ref, va_ref, ga_ref, qb_ref, kb_ref, vb_ref, gb_ref)):
        u = _dot(xb, wm_ref[:, c * WIDTH:(c + 1) * WIDTH])
        if c % 4 == 0:
            u = u * (1.0 / 8.0)
        o_ref[...] = u.astype(o_ref.dtype)
    logf = _log_sigmoid(_dot(xb, wf_ref[...]) + bf_ref[...])
    lf_ref[...] = logf[:, 0:HEADS]
    lane = lax.broadcasted_iota(jnp.int32, (tm, LANES), 1)
    hi, mid, lo = _split3(jnp.where(lane < 3 * HEADS, logf, 0.0))
    tri = tri_ref[...]
    c = _dot(tri, hi) + _dot(tri, mid) + _dot(tri, lo)
    h, m, l = _split3(-c)
    zero = jnp.zeros_like(h)
    cs_ref[...] = jnp.where(lane < 8, h, jnp.where(lane < 16, m, jnp.where(lane < 24, l, zero)))


def _project_sample(x, g, wm, wf, bf, tri, tm):
    rows = x.shape[0]
    row = lambda w: pl.BlockSpec((tm, w), lambda t: (t, 0))
    const = lambda s: pl.BlockSpec(s, lambda t: (0,) * len(s))
    slab_dtypes = [BF16, F32, F32, F32, BF16, F32, F32, F32]
    out_shape = ([jax.ShapeDtypeStruct((rows, WIDTH), dt) for dt in slab_dtypes]
                 + [jax.ShapeDtypeStruct((rows, HEADS), F32), jax.ShapeDtypeStruct((rows, LANES), BF16)])
    return pl.pallas_call(
        _proj_sample_body,
        grid=(rows // tm,),
        in_specs=[row(D_MODEL), const((1, D_MODEL)), const(wm.shape), const(wf.shape),
                  const((1, LANES)), const((tm, tm))],
        out_specs=[row(WIDTH)] * 8 + [row(HEADS), row(LANES)],
        out_shape=out_shape,
        compiler_params=pltpu.CompilerParams(dimension_semantics=("arbitrary",),
                                             vmem_limit_bytes=VMEM_LIMIT),
        name="project_sample",
    )(x, g, wm, wf, bf, tri)


def _sb_suffix(z, sp, tt, carry, valid):
    hi = sp.astype(BF16)
    lo = (sp - hi.astype(F32)).astype(BF16)
    cs = _dot(jnp.concatenate([hi, lo], axis=1), tt)
    w = jnp.exp(z - cs[:, :SUB] - carry)
    if valid is not None:
        w = jnp.where(valid, w, 0.0)
    return w.astype(BF16), carry + cs[:, SUB:]


def _softplus(z, valid):
    sp = jnp.maximum(z, 0.0) + jnp.log(1.0 + jnp.exp2(jnp.abs(z) * (-LOG2_E)))
    if valid is not None:
        sp = jnp.where(valid, sp, 0.0)
    return sp


def _sb_chunk(qv, kt, vt, tt, carry, acc, valid):
    z = _dot(qv, kt)
    sp = _softplus(z, valid)
    ws = []
    for u in reversed(range(z.shape[1] // SUB)):
        sl = slice(u * SUB, (u + 1) * SUB)
        w, carry = _sb_suffix(z[:, sl], sp[:, sl], tt, carry, None if valid is None else valid[:, sl])
        ws.append(w)
    acc = acc + _dot_nt(jnp.concatenate(ws[::-1], axis=1), vt)
    return carry, acc


def _sb_rows(qv, k, v, tt, carry, acc, valid):
    z = _dot_nt(qv, k)
    w, carry = _sb_suffix(z, _softplus(z, valid), tt, carry, valid)
    return carry, acc + _dot(w, v)


def _softmax_step(s, m, l, acc, valid, pv):
    if valid is not None:
        s = jnp.where(valid, s, NEG_BIG)
    m_new = jnp.maximum(m, jnp.max(s, axis=-1, keepdims=True))
    alpha = jnp.exp(m - m_new)
    p = jnp.exp(s - m_new)
    l = alpha * l + jnp.sum(p, axis=-1, keepdims=True)
    acc = alpha * acc + pv(p.astype(BF16))
    return m_new, l, acc


def _one_hot_bias_cols(head, shape):
    col = lax.broadcasted_iota(jnp.int32, shape, 1)
    hit = (col == head) | (col == head + HEADS) | (col == head + 2 * HEADS)
    return jnp.where(hit, 1.0, 0.0).astype(BF16)


def _shifted(src, stage):
    r, length = src.shape
    tail = (length // LANES) * LANES
    stage[0:r, tail:] = jnp.zeros((r, stage.shape[1] - tail), F32)
    stage[0:r, 0:length] = src
    return pltpu.roll(stage[0:r, :], PAD_KEYS, axis=1)


def _prompt_geometry(chunk):
    rows = 2 * Q_ROWS
    row = lax.broadcasted_iota(jnp.int32, (rows, chunk), 0)
    col = lax.broadcasted_iota(jnp.int32, (rows, chunk), 1)
    qidx = row & (Q_ROWS - 1)
    r128 = lax.broadcasted_iota(jnp.int32, (rows, LANES), 0)
    c128 = lax.broadcasted_iota(jnp.int32, (rows, LANES), 1)
    head_mask = (r128 < Q_ROWS) == (c128 < HEAD_DIM)
    first_head_lanes = lax.broadcasted_iota(jnp.int32, (Q_ROWS, LANES), 1) < HEAD_DIM
    return col, qidx, head_mask, first_head_lanes


def _prompt_q_loop(n_qblocks, n_streams, q_ref, g_ref, o_ref, head_mask, first_head_lanes, attend):
    def q_block(qi, _):
        qs = pl.multiple_of(N_META + qi * Q_ROWS, 16)
        q = q_ref[0, pl.ds(qs, Q_ROWS), :]
        qvs = []
        for s in range(n_streams):
            qp = q[:, s * LANES:(s + 1) * LANES]
            qvs.append(jnp.where(head_mask, jnp.concatenate([qp, qp], axis=0), jnp.zeros((), BF16)))
        outs = [jnp.where(first_head_lanes, o[:Q_ROWS], o[Q_ROWS:]) for o in attend(qi, qvs)]
        g = g_ref[0, pl.ds(qs, Q_ROWS), :]
        os_ = pl.multiple_of(qi * Q_ROWS, Q_ROWS)
        o_ref[0, pl.ds(os_, Q_ROWS), :] = (jnp.concatenate(outs, axis=1) * _silu(g)).astype(o_ref.dtype)
        return 0

    lax.fori_loop(0, n_qblocks, q_block, 0)


def _chunk_plan(qi, chunk):
    blk = qi + 1
    per = chunk // SUB
    return blk // per, (blk % per) * SUB


def _chunk_ref(ref, r0, nrows, c, chunk):
    return ref[r0:r0 + nrows, pl.ds(pl.multiple_of(c * chunk, chunk), chunk)]


def _sb_prompt_body(q_ref, kt_ref, vt_ref, g_ref, tt_ref, o_ref, k16, v16, stage, *, n_qblocks, n_streams):
    for s in range(n_streams):
        rs = slice(s * LANES, (s + 1) * LANES)
        k16[rs, :] = _shifted(kt_ref[0, rs, :], stage).astype(BF16)
        v16[rs, :] = _shifted(vt_ref[0, rs, :], stage).astype(BF16)
    col, qidx, head_mask, first_head_lanes = _prompt_geometry(CHUNK)
    tt = tt_ref[...]
    rows = 2 * Q_ROWS

    def attend(qi, qvs):
        last, tile_lane = _chunk_plan(qi, CHUNK)

        def chunk(c, state, valid):
            return tuple(
                _sb_chunk(qvs[s], _chunk_ref(k16, s * LANES, LANES, c, CHUNK),
                          _chunk_ref(v16, s * LANES, LANES, c, CHUNK), tt, carry, acc, valid)
                for s, (carry, acc) in enumerate(state))

        def live(state):
            low = state[0][0]
            for carry, _ in state[1:]:
                low = jnp.minimum(low, carry)
            return jnp.min(low) < SB_DEAD_CARRY

        zero = jnp.zeros((rows, LANES), F32)
        state = chunk(last, ((zero, zero),) * n_streams, col - tile_lane < qidx)

        def older(loop):
            j, _, state = loop
            state = chunk(last - 1 - j, state, None)
            return j + 1, live(state), state

        _, _, state = lax.while_loop(lambda loop: (loop[0] < last) & loop[1], older,
                                     (jnp.int32(0), live(state), state))
        return [acc for _, acc in state]

    _prompt_q_loop(n_qblocks, n_streams, q_ref, g_ref, o_ref, head_mask, first_head_lanes, attend)


def _fox_prompt_body(q_ref, kt_ref, vt_ref, cst_ref, g_ref, o_ref, kc16, v16, stage, *, n_qblocks, n_streams):
    bias = _shifted(cst_ref[0].astype(F32), stage)
    brow = lax.broadcasted_iota(jnp.int32, bias.shape, 0)
    blane = lax.broadcasted_iota(jnp.int32, bias.shape, 1)
    bias = jnp.where((blane < PAD_KEYS) & (brow < HEADS), NEG_BIG, bias).astype(BF16)
    kc_rows = 2 * LANES
    for s in range(n_streams):
        rs = slice(s * LANES, (s + 1) * LANES)
        kc16[s * kc_rows:s * kc_rows + LANES, :] = _shifted(kt_ref[0, rs, :], stage).astype(BF16)
        kc16[s * kc_rows + LANES:s * kc_rows + LANES + BIAS_ROWS, :] = bias
        kc16[s * kc_rows + LANES + BIAS_ROWS:(s + 1) * kc_rows, :] = jnp.zeros(
            (LANES - BIAS_ROWS, kc16.shape[1]), BF16)
        v16[rs, :] = _shifted(vt_ref[0, rs, :], stage).astype(BF16)
    col, qidx, head_mask, first_head_lanes = _prompt_geometry(FOX_CHUNK)
    rows = 2 * Q_ROWS
    r128 = lax.broadcasted_iota(jnp.int32, (rows, LANES), 0)
    first_head = 2 * n_streams * pl.program_id(1) + jnp.where(r128 < Q_ROWS, 0, 1)
    qones = [_one_hot_bias_cols(first_head + 2 * s, (rows, LANES)) for s in range(n_streams)]

    def attend(qi, qvs):
        last, tile_lane = _chunk_plan(qi, FOX_CHUNK)
        qv2s = [jnp.concatenate([qv, qone], axis=1) for qv, qone in zip(qvs, qones)]

        def chunk(c, state, valid):
            out = []
            for s, (m, l, acc) in enumerate(state):
                vt = _chunk_ref(v16, s * LANES, LANES, c, FOX_CHUNK)
                sc = _dot(qv2s[s], _chunk_ref(kc16, s * kc_rows, kc_rows, c, FOX_CHUNK))
                out.append(_softmax_step(sc, m, l, acc, valid, lambda p, vt=vt: _dot_nt(p, vt)))
            return tuple(out)

        init = (jnp.full((rows, 1), NEG_BIG, F32), jnp.zeros((rows, 1), F32), jnp.zeros((rows, LANES), F32))
        state = chunk(last, (init,) * n_streams, col - tile_lane <= qidx)
        state = lax.fori_loop(0, last, lambda j, st: chunk(last - 1 - j, st, None), state)
        return [acc / l for _, l, acc in state]

    _prompt_q_loop(n_qblocks, n_streams, q_ref, g_ref, o_ref, head_mask, first_head_lanes, attend)


def _prompt_attention(kind, q, kt, vt, g, extra, seq):
    b, length, _ = q.shape
    n_qblocks = seq // Q_ROWS
    chunk = CHUNK if kind == "sb" else FOX_CHUNK
    padded = ((PAD_KEYS + length + chunk - 1) // chunk) * chunk
    ns = PROMPT_STREAMS
    wide = ns * LANES
    tok = pl.BlockSpec((1, length, wide), lambda i, p: (i, 0, p))
    feat = pl.BlockSpec((1, wide, length), lambda i, p: (i, p, 0))
    out_spec = pl.BlockSpec((1, seq, wide), lambda i, p: (i, 0, p))
    stage = pltpu.VMEM((LANES, padded), F32)
    if kind == "sb":
        body = functools.partial(_sb_prompt_body, n_qblocks=n_qblocks, n_streams=ns)
        in_specs = [tok, feat, feat, tok, pl.BlockSpec(extra.shape, lambda i, p: (0, 0))]
        args = (q, kt, vt, g, extra)
        scratch = [pltpu.VMEM((wide, padded), BF16), pltpu.VMEM((wide, padded), BF16), stage]
    else:
        body = functools.partial(_fox_prompt_body, n_qblocks=n_qblocks, n_streams=ns)
        in_specs = [tok, feat, feat, pl.BlockSpec((1, BIAS_ROWS, length), lambda i, p: (i, 0, 0)), tok]
        args = (q, kt, vt, extra, g)
        scratch = [pltpu.VMEM((2 * wide, padded), BF16), pltpu.VMEM((wide, padded), BF16), stage]
    return pl.pallas_call(
        body,
        grid=(b, WIDTH // wide),
        in_specs=in_specs,
        out_specs=out_spec,
        out_shape=jax.ShapeDtypeStruct((b, seq, WIDTH), BF16),
        scratch_shapes=scratch,
        compiler_params=pltpu.CompilerParams(dimension_semantics=("arbitrary", "arbitrary"),
                                             vmem_limit_bytes=VMEM_LIMIT),
        name=kind + "_prompt",
    )(*args)


def _sample_setup(q_ref, qv_s, dec):
    q = q_ref[0]
    lane_head = lax.broadcasted_iota(jnp.int32, q.shape, 1) >> 6
    for h in range(HEADS):
        qv_s[h * dec:(h + 1) * dec, :] = jnp.where(lane_head == h, q, jnp.zeros((), BF16))


def _pad_rows_bf16(x, rows):
    x = x.astype(BF16)
    return jnp.concatenate([x, jnp.zeros((rows - x.shape[0], x.shape[1]), BF16)], axis=0)


def _gather_heads(acc, dec):
    lane_head = lax.broadcasted_iota(jnp.int32, (dec, WIDTH), 1) >> 6
    out = jnp.zeros((dec, WIDTH), F32)
    for h in range(HEADS):
        out = jnp.where(lane_head == h, acc[h * dec:(h + 1) * dec, :], out)
    return out


def _sb_sample_body(q_ref, kn_ref, vn_ref, g_ref, ckt_ref, cvt_ref, tt_ref, o_ref,
                    qv_s, carry_s, acc_s, *, dec):
    j = pl.program_id(1)
    rows = HEADS * dec
    tt = tt_ref[...]

    @pl.when(j == 0)
    def _():
        _sample_setup(q_ref, qv_s, dec)
        row = lax.broadcasted_iota(jnp.int32, (rows, SUB), 0)
        col = lax.broadcasted_iota(jnp.int32, (rows, SUB), 1)
        carry, acc = _sb_rows(qv_s[...], _pad_rows_bf16(kn_ref[0], SUB), _pad_rows_bf16(vn_ref[0], SUB),
                              tt, jnp.zeros((rows, SUB), F32), jnp.zeros((rows, WIDTH), F32),
                              col < (row & (dec - 1)))
        carry_s[...] = carry
        acc_s[...] = acc

    carry, acc = _sb_chunk(qv_s[...], ckt_ref[0].astype(BF16), cvt_ref[0].astype(BF16), tt,
                           carry_s[...], acc_s[...], None)
    carry_s[...] = carry
    acc_s[...] = acc

    @pl.when(j == pl.num_programs(1) - 1)
    def _():
        o_ref[0] = (_gather_heads(acc, dec) * _silu(g_ref[0])).astype(o_ref.dtype)


def _fox_sample_body(q_ref, kn_ref, vn_ref, csn_ref, g_ref, ckt_ref, cvt_ref, bh_ref, bm_ref, bl_ref,
                     o_ref, qv_s, m_s, l_s, acc_s, *, dec):
    j = pl.program_id(1)
    rows = HEADS * dec
    head = lax.broadcasted_iota(jnp.int32, (rows, 1), 0) >> (dec.bit_length() - 1)

    @pl.when(j == 0)
    def _():
        _sample_setup(q_ref, qv_s, dec)
        row = lax.broadcasted_iota(jnp.int32, (rows, SUB), 0)
        col = lax.broadcasted_iota(jnp.int32, (rows, SUB), 1)
        vn = _pad_rows_bf16(vn_ref[0], SUB)
        s = (_dot_nt(qv_s[...], _pad_rows_bf16(kn_ref[0], SUB))
             + _dot_nt(_one_hot_bias_cols(head, (rows, LANES)), _pad_rows_bf16(csn_ref[0], SUB)))
        m, l, acc = _softmax_step(s, jnp.full((rows, 1), NEG_BIG, F32), jnp.zeros((rows, 1), F32),
                                  jnp.zeros((rows, WIDTH), F32), col <= (row & (dec - 1)),
                                  lambda p: _dot(p, vn))
        m_s[...] = m
        l_s[...] = l
        acc_s[...] = acc

    tkb = ckt_ref.shape[2]
    bias = jnp.concatenate([bh_ref[0], bm_ref[0], bl_ref[0], jnp.zeros((HEADS, tkb), F32)],
                           axis=0).astype(BF16)
    vt = cvt_ref[0].astype(BF16)
    s = _dot(qv_s[...], ckt_ref[0].astype(BF16)) + _dot(_one_hot_bias_cols(head, (rows, BIAS_ROWS)), bias)
    m, l, acc = _softmax_step(s, m_s[...], l_s[...], acc_s[...], None, lambda p: _dot_nt(p, vt))
    m_s[...] = m
    l_s[...] = l
    acc_s[...] = acc

    @pl.when(j == pl.num_programs(1) - 1)
    def _():
        o_ref[0] = (_gather_heads(acc / l, dec) * _silu(g_ref[0])).astype(o_ref.dtype)


def _sample_attention(kind, q, kn, vn, g, cache_kt, cache_vt, extra, tkb):
    b, dec, _ = q.shape
    past = cache_kt.shape[2]
    nkb = past // tkb
    rows = HEADS * dec
    new = lambda w: pl.BlockSpec((1, dec, w), lambda i, j: (i, 0, 0))
    old = lambda r: pl.BlockSpec((1, r, tkb), lambda i, j: (i, 0, nkb - 1 - j))
    if kind == "sb":
        body = functools.partial(_sb_sample_body, dec=dec)
        in_specs = [new(WIDTH), new(WIDTH), new(WIDTH), new(WIDTH), old(WIDTH), old(WIDTH),
                    pl.BlockSpec(extra.shape, lambda i, j: (0, 0))]
        args = (q, kn, vn, g, cache_kt, cache_vt, extra)
        scratch = [pltpu.VMEM((rows, WIDTH), BF16), pltpu.VMEM((rows, SUB), F32),
                   pltpu.VMEM((rows, WIDTH), F32)]
    else:
        csn, bh, bm, bl = extra
        body = functools.partial(_fox_sample_body, dec=dec)
        in_specs = [new(WIDTH), new(WIDTH), new(WIDTH), new(LANES), new(WIDTH), old(WIDTH), old(WIDTH),
                    old(HEADS), old(HEADS), old(HEADS)]
        args = (q, kn, vn, csn, g, cache_kt, cache_vt, bh, bm, bl)
        scratch = [pltpu.VMEM((rows, WIDTH), BF16), pltpu.VMEM((rows, 1), F32),
                   pltpu.VMEM((rows, 1), F32), pltpu.VMEM((rows, WIDTH), F32)]
    return pl.pallas_call(
        body,
        grid=(b, nkb),
        in_specs=in_specs,
        out_specs=new(WIDTH),
        out_shape=jax.ShapeDtypeStruct((b, dec, WIDTH), BF16),
        scratch_shapes=scratch,
        compiler_params=pltpu.CompilerParams(dimension_semantics=("arbitrary", "arbitrary"),
                                             vmem_limit_bytes=VMEM_LIMIT),
        name=kind + "_sample",
    )(*args)


def _out_body(ma_ref, mb_ref, x_ref, wa_ref, wb_ref, g_ref, y_ref):
    h = x_ref[...] + _dot(ma_ref[...], wa_ref[...]) + _dot(mb_ref[...], wb_ref[...])
    ms = jnp.mean(h * h, axis=-1, keepdims=True)
    y_ref[...] = (h * lax.rsqrt(ms + EPS)) * g_ref[...]


def _output(ma, mb, x, wa, wb, g, tm):
    r = x.shape[0]
    row = lambda w: pl.BlockSpec((tm, w), lambda i: (i, 0))
    const = lambda s: pl.BlockSpec(s, lambda i: (0, 0))
    return pl.pallas_call(
        _out_body,
        grid=(r // tm,),
        in_specs=[row(WIDTH), row(WIDTH), row(D_MODEL), const(wa.shape), const(wb.shape),
                  const((1, D_MODEL))],
        out_specs=row(D_MODEL),
        out_shape=jax.ShapeDtypeStruct((r, D_MODEL), F32),
        compiler_params=pltpu.CompilerParams(dimension_semantics=("arbitrary",),
                                             vmem_limit_bytes=VMEM_LIMIT),
        name="output",
    )(ma, mb, x, wa, wb, g)


def _tri(n, cmp, group=None):
    i = np.arange(n)
    t = cmp(i[:, None], i[None, :])
    if group is not None:
        t &= (i[:, None] // group) == (i[None, :] // group)
    return t.astype(np.float32)


def _suffix_sum_matrix():
    t = _tri(SUB, np.greater_equal)
    half = np.concatenate([t, np.ones_like(t)], axis=1)
    return jnp.asarray(np.concatenate([half, half], axis=0), dtype=BF16)


def kernel(x_prompt, x_sample, cache_a_k, cache_a_v, cache_b_k, cache_b_v, cache_b_logf,
           meta_tokens, norm_g, w_in, b_f, w_out, final_g):
    depth = norm_g.shape[0]
    assert depth == 1, "single-layer trunk only"
    batch, seq, _ = x_prompt.shape
    dec_batch, dec_seq, _ = x_sample.shape
    past = cache_a_k.shape[2]
    length = N_META + seq
    tm = 256

    w = w_in[0]
    wt_all = jnp.transpose(w)
    slab = lambda a, c: a[c * WIDTH:(c + 1) * WIDTH]
    wm = w[:, :8 * WIDTH].astype(BF16)
    wrow = jnp.concatenate([w[:, c * WIDTH:(c + 1) * WIDTH] for c in (0, 3, 4, 7)], axis=1).astype(BF16)
    wt = jnp.concatenate([slab(wt_all, c) for c in (1, 2, 5, 6)], axis=0).astype(BF16)
    wft = jnp.pad(wt_all[8 * WIDTH:], ((0, 16 - HEADS), (0, 0))).astype(BF16)
    bft = jnp.broadcast_to(jnp.pad(b_f[0], (0, 16 - HEADS))[:, None], (16, LANES)).astype(F32)
    wf = jnp.pad(jnp.tile(w[:, 8 * WIDTH:], (1, 3)), ((0, 0), (0, LANES - 3 * HEADS))).astype(BF16)
    bf = jnp.pad(jnp.tile(b_f[0], 3), (0, LANES - 3 * HEADS)).reshape(1, LANES).astype(F32)
    g_in = norm_g[0].reshape(1, D_MODEL)
    g_out = final_g.reshape(1, D_MODEL)
    wa = w_out[0, :WIDTH].astype(BF16)
    wb = w_out[0, WIDTH:].astype(BF16)
    tt = _suffix_sum_matrix()
    triu = jnp.asarray(_tri(tm, np.less_equal), dtype=BF16)
    tri_s = jnp.asarray(_tri(tm, np.greater_equal, group=dec_seq), dtype=BF16)
    suffix_w = jnp.asarray(np.concatenate([_tri(tm, np.greater), np.ones((tm, LANES), np.float32)],
                                          axis=1), dtype=BF16)

    rows_in = -(-length // tm) * tm
    meta = jnp.broadcast_to(meta_tokens[None].astype(x_prompt.dtype), (batch, N_META, D_MODEL))
    hp = jnp.concatenate([meta, x_prompt, jnp.zeros((batch, rows_in - length, D_MODEL), x_prompt.dtype)],
                         axis=1)
    (qa, ga, qb, gb, kat, vat, kbt, vbt, lft, cst) = _project_prompt(
        hp, length, g_in, wrow, wt, wft, bft, triu, tm)
    mixed_a = _prompt_attention("sb", qa, kat, vat, ga, tt, seq)
    mixed_b = _prompt_attention("fox", qb, kbt, vbt, gb, cst, seq)
    y_prompt = _output(mixed_a.reshape(batch * seq, WIDTH), mixed_b.reshape(batch * seq, WIDTH),
                       x_prompt.reshape(batch * seq, D_MODEL), wa, wb, g_out, tm).reshape(batch, seq, D_MODEL)

    s_rows = dec_batch * dec_seq
    (sqa, ska, sva, sga, sqb, skb, svb, sgb, slf, scs) = _project_sample(
        x_sample.reshape(s_rows, D_MODEL), g_in, wm, wf, bf, tri_s, tm)
    clft = jnp.transpose(cache_b_logf[0], (0, 2, 1)).reshape(dec_batch * HEADS, past)
    bias_parts = [p.reshape(dec_batch, HEADS, past) for p in _cache_bias(clft, suffix_w, tm)]
    per_b = lambda a: a.reshape(dec_batch, dec_seq, a.shape[-1])
    feat_major = lambda c: jnp.transpose(c[0], (0, 2, 3, 1)).reshape(dec_batch, WIDTH, past)
    smix_a = _sample_attention("sb", per_b(sqa), per_b(ska), per_b(sva), per_b(sga),
                               feat_major(cache_a_k), feat_major(cache_a_v), tt, CHUNK)
    smix_b = _sample_attention("fox", per_b(sqb), per_b(skb), per_b(svb), per_b(sgb),
                               feat_major(cache_b_k), feat_major(cache_b_v), (per_b(scs), *bias_parts), CHUNK)
    y_sample = _output(smix_a.reshape(s_rows, WIDTH), smix_b.reshape(s_rows, WIDTH),
                       x_sample.reshape(s_rows, D_MODEL), wa, wb, g_out, tm).reshape(dec_batch, dec_seq, D_MODEL)

    def prompt_heads(at):
        return jnp.transpose(at.reshape(batch, HEADS, HEAD_DIM, length), (0, 3, 1, 2))[None]

    sample_heads = lambda a: a.reshape(1, dec_batch, dec_seq, HEADS, HEAD_DIM)
    return (y_prompt, y_sample,
            prompt_heads(kat), prompt_heads(vat), prompt_heads(kbt), prompt_heads(vbt),
            jnp.transpose(lft, (0, 2, 1))[None],
            sample_heads(ska), sample_heads(sva), sample_heads(skb), sample_heads(svb),
            slf.reshape(1, dec_batch, dec_seq, HEADS))
```

```python
import functools

import jax
import jax.numpy as jnp
import numpy as np
from jax import lax
from jax.experimental import pallas as pl
from jax.experimental.pallas import tpu as pltpu

F32 = jnp.float32
BF16 = jnp.bfloat16

D_MODEL = 1024
HEAD_DIM = 64
HEADS = 8
WIDTH = HEADS * HEAD_DIM
N_META = 16
EPS = 1e-6
LANES = 128
SUB = 128
CHUNK = 512
FOX_CHUNK = 1024
SAMPLE_CHUNK = 1024
Q_ROWS = 128
FOX_Q_ROWS = 256
LOG2_E = 1.4426950408889634
SB_DEAD_CARRY = 104.0
BIAS_ROWS = 32
PAD_KEYS = SUB - N_META
PROMPT_STREAMS = 2
NEG_BIG = -1e30
VMEM_LIMIT = 56 * 1024 * 1024


def _split3(x):
    hi = x.astype(BF16)
    r1 = x - hi.astype(F32)
    mid = r1.astype(BF16)
    lo = (r1 - mid.astype(F32)).astype(BF16)
    return hi, mid, lo


def _log_sigmoid(x):
    return jnp.minimum(x, 0.0) - jnp.log(1.0 + jnp.exp(-jnp.abs(x)))


def _silu(g):
    return g / (1.0 + jnp.exp(-g))


def _dot(a, b):
    return jnp.dot(a, b, preferred_element_type=F32)


def _dot_nt(a, b):
    return lax.dot_general(a, b, (((1,), (1,)), ((), ())), preferred_element_type=F32)


def _rms_normed(x, g):
    ms = jnp.mean(x * x, axis=-1, keepdims=True)
    return ((x * lax.rsqrt(ms + EPS)) * g).astype(BF16)


def _cache_bias_body(x_ref, w_ref, hi_ref, mid_ref, lo_ref, carry_ref):
    @pl.when(pl.program_id(0) == 0)
    def _():
        carry_ref[...] = jnp.zeros_like(carry_ref)

    x = x_ref[...]
    tb = x.shape[1]
    hi, mid, lo = _split3(x)
    w = w_ref[...]
    r = _dot(hi, w) + _dot(mid, w) + _dot(lo, w)
    carry = carry_ref[...]
    s = r[:, :tb] + jnp.concatenate([carry] * (tb // LANES), axis=1)
    carry_ref[...] = carry + r[:, tb:]
    h, m, l = _split3(s)
    hi_ref[...] = h.astype(F32)
    mid_ref[...] = m.astype(F32)
    lo_ref[...] = l.astype(F32)


def _cache_bias(x, w, tb):
    rows, past = x.shape
    nb = past // tb
    blk = pl.BlockSpec((rows, tb), lambda t: (0, nb - 1 - t))
    return pl.pallas_call(
        _cache_bias_body,
        grid=(nb,),
        in_specs=[blk, pl.BlockSpec(w.shape, lambda t: (0, 0))],
        out_specs=[blk, blk, blk],
        out_shape=[jax.ShapeDtypeStruct((rows, past), F32)] * 3,
        scratch_shapes=[pltpu.VMEM((rows, LANES), F32)],
        compiler_params=pltpu.CompilerParams(dimension_semantics=("arbitrary",),
                                             vmem_limit_bytes=VMEM_LIMIT),
        name="cache_bias",
    )(x, w)


def _proj_prompt_body(x_ref, g_ref, wrow_ref, wt_ref, wft_ref, bft_ref, triu_ref,
                      qa_ref, ga_ref, qb_ref, gb_ref, kat_ref, vat_ref, kbt_ref, vbt_ref,
                      lft_ref, cst_ref, carry_ref):
    t = pl.program_id(1)
    xb = _rms_normed(x_ref[0], g_ref[...])
    tm = xb.shape[0]
    for c, o_ref in enumerate((qa_ref, ga_ref, qb_ref, gb_ref)):
        u = _dot(xb, wrow_ref[:, c * WIDTH:(c + 1) * WIDTH])
        if c % 2 == 0:
            u = u * (1.0 / 8.0)
        o_ref[0] = u.astype(o_ref.dtype)
    for c, o_ref in enumerate((kat_ref, vat_ref, kbt_ref, vbt_ref)):
        o_ref[0] = _dot_nt(wt_ref[c * WIDTH:(c + 1) * WIDTH, :], xb)

    reps = tm // LANES
    flt = _dot_nt(wft_ref[...], xb) + jnp.concatenate([bft_ref[...]] * reps, axis=1)
    logf = _log_sigmoid(flt)
    lft_ref[0] = logf[0:HEADS]
    hi, mid, lo = _split3(logf)
    triu = triu_ref[...]
    c = _dot(hi, triu) + _dot(mid, triu) + _dot(lo, triu)

    @pl.when(t == 0)
    def _():
        carry_ref[...] = jnp.zeros_like(carry_ref)

    c = c + jnp.concatenate([carry_ref[...]] * reps, axis=1)
    carry_ref[...] = jnp.broadcast_to(c[:, tm - 1:tm], carry_ref.shape)
    h, m, l = _split3(-c[0:HEADS])
    cst_ref[0] = jnp.concatenate([h.astype(F32), m.astype(F32), l.astype(F32),
                                  jnp.zeros((HEADS, tm), F32)], axis=0).astype(BF16)


def _project_prompt(hp, length, g, wrow, wt, wft, bft, triu, tm):
    b, rows_in, _ = hp.shape
    nt = rows_in // tm
    const = lambda s: pl.BlockSpec(s, lambda i, t: (0,) * len(s))
    tok = lambda w: pl.BlockSpec((1, tm, w), lambda i, t: (i, t, 0))
    feat = lambda r: pl.BlockSpec((1, r, tm), lambda i, t: (i, 0, t))
    out_shape = ([jax.ShapeDtypeStruct((b, length, WIDTH), dt) for dt in (BF16, F32, BF16, F32)]
                 + [jax.ShapeDtypeStruct((b, WIDTH, length), F32)] * 4
                 + [jax.ShapeDtypeStruct((b, HEADS, length), F32),
                    jax.ShapeDtypeStruct((b, BIAS_ROWS, length), BF16)])
    return pl.pallas_call(
        _proj_prompt_body,
        grid=(b, nt),
        in_specs=[tok(D_MODEL), const((1, D_MODEL)), const(wrow.shape), const(wt.shape),
                  const(wft.shape), const(bft.shape), const(triu.shape)],
        out_specs=[tok(WIDTH)] * 4 + [feat(WIDTH)] * 4 + [feat(HEADS), feat(BIAS_ROWS)],
        out_shape=out_shape,
        scratch_shapes=[pltpu.VMEM((16, LANES), F32)],
        compiler_params=pltpu.CompilerParams(dimension_semantics=("arbitrary", "arbitrary"),
                                             vmem_limit_bytes=VMEM_LIMIT),
        name="project_prompt",
    )(hp, g, wrow, wt, wft, bft, triu)


def _proj_sample_body(x_ref, g_ref, wm_ref, wf_ref, bf_ref, tri_ref,
                      qa_ref, ka_ref, va_ref, ga_ref, qb_ref, kb_ref, vb_ref, gb_ref, lf_ref, cs_ref):
    xb = _rms_normed(x_ref[...], g_ref[...])
    tm = xb.shape[0]
    for c, o_ref in enumerate((qa_ref, ka_ref, va_ref, ga_ref, qb_ref, kb_ref, vb_ref, gb_ref)):
        u = _dot(xb, wm_ref[:, c * WIDTH:(c + 1) * WIDTH])
        if c % 4 == 0:
            u = u * (1.0 / 8.0)
        o_ref[...] = u.astype(o_ref.dtype)
    logf = _log_sigmoid(_dot(xb, wf_ref[...]) + bf_ref[...])
    lf_ref[...] = logf[:, 0:HEADS]
    lane = lax.broadcasted_iota(jnp.int32, (tm, LANES), 1)
    hi, mid, lo = _split3(jnp.where(lane < 3 * HEADS, logf, 0.0))
    tri = tri_ref[...]
    c = _dot(tri, hi) + _dot(tri, mid) + _dot(tri, lo)
    h, m, l = _split3(-c)
    zero = jnp.zeros_like(h)
    cs_ref[...] = jnp.where(lane < 8, h, jnp.where(lane < 16, m, jnp.where(lane < 24, l, zero)))


def _project_sample(x, g, wm, wf, bf, tri, tm):
    rows = x.shape[0]
    row = lambda w: pl.BlockSpec((tm, w), lambda t: (t, 0))
    const = lambda s: pl.BlockSpec(s, lambda t: (0,) * len(s))
    slab_dtypes = [BF16, F32, F32, F32, BF16, F32, F32, F32]
    out_shape = ([jax.ShapeDtypeStruct((rows, WIDTH), dt) for dt in slab_dtypes]
                 + [jax.ShapeDtypeStruct((rows, HEADS), F32), jax.ShapeDtypeStruct((rows, LANES), BF16)])
    return pl.pallas_call(
        _proj_sample_body,
        grid=(rows // tm,),
        in_specs=[row(D_MODEL), const((1, D_MODEL)), const(wm.shape), const(wf.shape),
                  const((1, LANES)), const((tm, tm))],
        out_specs=[row(WIDTH)] * 8 + [row(HEADS), row(LANES)],
        out_shape=out_shape,
        compiler_params=pltpu.CompilerParams(dimension_semantics=("arbitrary",),
                                             vmem_limit_bytes=VMEM_LIMIT),
        name="project_sample",
    )(x, g, wm, wf, bf, tri)


def _sb_suffix(z, sp, tt, carry, valid):
    hi = sp.astype(BF16)
    lo = (sp - hi.astype(F32)).astype(BF16)
    cs = _dot(jnp.concatenate([hi, lo], axis=1), tt)
    w = jnp.exp(z - cs[:, :SUB] - carry)
    if valid is not None:
        w = jnp.where(valid, w, 0.0)
    return w.astype(BF16), carry + cs[:, SUB:]


def _softplus(z, valid):
    sp = jnp.maximum(z, 0.0) + jnp.log(1.0 + jnp.exp2(jnp.abs(z) * (-LOG2_E)))
    if valid is not None:
        sp = jnp.where(valid, sp, 0.0)
    return sp


def _sb_chunk(qv, kt, vt, tt, carry, acc, valid):
    z = _dot(qv, kt)
    sp = _softplus(z, valid)
    ws = []
    for u in reversed(range(z.shape[1] // SUB)):
        sl = slice(u * SUB, (u + 1) * SUB)
        w, carry = _sb_suffix(z[:, sl], sp[:, sl], tt, carry, None if valid is None else valid[:, sl])
        ws.append(w)
    acc = acc + _dot_nt(jnp.concatenate(ws[::-1], axis=1), vt)
    return carry, acc


def _sb_rows(qv, k, v, tt, carry, acc, valid):
    z = _dot_nt(qv, k)
    w, carry = _sb_suffix(z, _softplus(z, valid), tt, carry, valid)
    return carry, acc + _dot(w, v)


def _softmax_step(s, m, l, acc, valid, pv):
    if valid is not None:
        s = jnp.where(valid, s, NEG_BIG)
    m_new = jnp.maximum(m, jnp.max(s, axis=-1, keepdims=True))
    alpha = jnp.exp(m - m_new)
    p = jnp.exp(s - m_new)
    l = alpha * l + jnp.sum(p, axis=-1, keepdims=True)
    acc = alpha * acc + pv(p.astype(BF16))
    return m_new, l, acc


def _one_hot_bias_cols(head, shape):
    col = lax.broadcasted_iota(jnp.int32, shape, 1)
    hit = (col == head) | (col == head + HEADS) | (col == head + 2 * HEADS)
    return jnp.where(hit, 1.0, 0.0).astype(BF16)


def _shifted(src, stage, pad):
    r, length = src.shape
    tail = (length // LANES) * LANES
    stage[0:r, tail:] = jnp.zeros((r, stage.shape[1] - tail), F32)
    stage[0:r, 0:length] = src
    return pltpu.roll(stage[0:r, :], pad, axis=1)


def _prompt_geometry(chunk, q_rows):
    rows = 2 * q_rows
    row = lax.broadcasted_iota(jnp.int32, (rows, chunk), 0)
    col = lax.broadcasted_iota(jnp.int32, (rows, chunk), 1)
    qidx = row & (q_rows - 1)
    r128 = lax.broadcasted_iota(jnp.int32, (rows, LANES), 0)
    c128 = lax.broadcasted_iota(jnp.int32, (rows, LANES), 1)
    head_mask = (r128 < q_rows) == (c128 < HEAD_DIM)
    first_head_lanes = lax.broadcasted_iota(jnp.int32, (q_rows, LANES), 1) < HEAD_DIM
    return col, qidx, head_mask, first_head_lanes


def _prompt_q_loop(q_rows, n_qblocks, n_streams, q_ref, g_ref, o_ref, head_mask, first_head_lanes, attend):
    def q_block(qi, _):
        qs = pl.multiple_of(N_META + qi * q_rows, 16)
        q = q_ref[0, pl.ds(qs, q_rows), :]
        qvs = []
        for s in range(n_streams):
            qp = q[:, s * LANES:(s + 1) * LANES]
            qvs.append(jnp.where(head_mask, jnp.concatenate([qp, qp], axis=0), jnp.zeros((), BF16)))
        outs = [jnp.where(first_head_lanes, o[:q_rows], o[q_rows:]) for o in attend(qi, qvs)]
        g = g_ref[0, pl.ds(qs, q_rows), :]
        os_ = pl.multiple_of(qi * q_rows, q_rows)
        o_ref[0, pl.ds(os_, q_rows), :] = (jnp.concatenate(outs, axis=1) * _silu(g)).astype(o_ref.dtype)
        return 0

    lax.fori_loop(0, n_qblocks, q_block, 0)


def _chunk_plan(qi, chunk):
    blk = qi + 1
    per = chunk // SUB
    return blk // per, (blk % per) * SUB


def _chunk_ref(ref, r0, nrows, c, chunk):
    return ref[r0:r0 + nrows, pl.ds(pl.multiple_of(c * chunk, chunk), chunk)]


def _sb_prompt_body(q_ref, kt_ref, vt_ref, g_ref, tt_ref, o_ref, k16, v16, stage, *, n_qblocks, n_streams):
    for s in range(n_streams):
        rs = slice(s * LANES, (s + 1) * LANES)
        k16[rs, :] = _shifted(kt_ref[0, rs, :], stage, PAD_KEYS).astype(BF16)
        v16[rs, :] = _shifted(vt_ref[0, rs, :], stage, PAD_KEYS).astype(BF16)
    col, qidx, head_mask, first_head_lanes = _prompt_geometry(CHUNK, Q_ROWS)
    tt = tt_ref[...]
    rows = 2 * Q_ROWS

    def attend(qi, qvs):
        last, tile_lane = _chunk_plan(qi, CHUNK)

        def chunk(c, state, valid):
            return tuple(
                _sb_chunk(qvs[s], _chunk_ref(k16, s * LANES, LANES, c, CHUNK),
                          _chunk_ref(v16, s * LANES, LANES, c, CHUNK), tt, carry, acc, valid)
                for s, (carry, acc) in enumerate(state))

        def live(state):
            low = state[0][0]
            for carry, _ in state[1:]:
                low = jnp.minimum(low, carry)
            return jnp.min(low) < SB_DEAD_CARRY

        zero = jnp.zeros((rows, LANES), F32)
        state = chunk(last, ((zero, zero),) * n_streams, col - tile_lane < qidx)

        def older(loop):
            j, _, state = loop
            state = chunk(last - 1 - j, state, None)
            return j + 1, live(state), state

        _, _, state = lax.while_loop(lambda loop: (loop[0] < last) & loop[1], older,
                                     (jnp.int32(0), live(state), state))
        return [acc for _, acc in state]

    _prompt_q_loop(Q_ROWS, n_qblocks, n_streams, q_ref, g_ref, o_ref, head_mask, first_head_lanes, attend)


def _fox_prompt_body(q_ref, kt_ref, vt_ref, cst_ref, g_ref, o_ref, kc16, v16, stage, *, n_qblocks, n_streams):
    q_rows = FOX_Q_ROWS
    pad = q_rows - N_META
    bias = _shifted(cst_ref[0].astype(F32), stage, pad)
    brow = lax.broadcasted_iota(jnp.int32, bias.shape, 0)
    blane = lax.broadcasted_iota(jnp.int32, bias.shape, 1)
    bias = jnp.where((blane < pad) & (brow < HEADS), NEG_BIG, bias).astype(BF16)
    kc_rows = 2 * LANES
    for s in range(n_streams):
        rs = slice(s * LANES, (s + 1) * LANES)
        kc16[s * kc_rows:s * kc_rows + LANES, :] = _shifted(kt_ref[0, rs, :], stage, pad).astype(BF16)
        kc16[s * kc_rows + LANES:s * kc_rows + LANES + BIAS_ROWS, :] = bias
        kc16[s * kc_rows + LANES + BIAS_ROWS:(s + 1) * kc_rows, :] = jnp.zeros(
            (LANES - BIAS_ROWS, kc16.shape[1]), BF16)
        v16[rs, :] = _shifted(vt_ref[0, rs, :], stage, pad).astype(BF16)
    col, qidx, head_mask, first_head_lanes = _prompt_geometry(FOX_CHUNK, q_rows)
    rows = 2 * q_rows
    r128 = lax.broadcasted_iota(jnp.int32, (rows, LANES), 0)
    first_head = 2 * n_streams * pl.program_id(1) + jnp.where(r128 < q_rows, 0, 1)
    qones = [_one_hot_bias_cols(first_head + 2 * s, (rows, LANES)) for s in range(n_streams)]

    def attend(qi, qvs):
        tile_lane = (qi + 1) * q_rows
        last = tile_lane // FOX_CHUNK
        qv2s = [jnp.concatenate([qv, qone], axis=1) for qv, qone in zip(qvs, qones)]

        def chunk(c, state, valid):
            out = []
            for s, (m, l, acc) in enumerate(state):
                vt = _chunk_ref(v16, s * LANES, LANES, c, FOX_CHUNK)
                sc = _dot(qv2s[s], _chunk_ref(kc16, s * kc_rows, kc_rows, c, FOX_CHUNK))
                out.append(_softmax_step(sc, m, l, acc, valid, lambda p, vt=vt: _dot_nt(p, vt)))
            return tuple(out)

        init = (jnp.full((rows, 1), NEG_BIG, F32), jnp.zeros((rows, 1), F32), jnp.zeros((rows, LANES), F32))
        state = chunk(last, (init,) * n_streams, col + (last * FOX_CHUNK - tile_lane) <= qidx)
        state = lax.fori_loop(0, last, lambda j, st: chunk(last - 1 - j, st, None), state)
        return [acc / l for _, l, acc in state]

    _prompt_q_loop(q_rows, n_qblocks, n_streams, q_ref, g_ref, o_ref, head_mask, first_head_lanes, attend)


def _prompt_attention(kind, q, kt, vt, g, extra, seq):
    b, length, _ = q.shape
    q_rows = Q_ROWS if kind == "sb" else FOX_Q_ROWS
    n_qblocks = seq // q_rows
    chunk = CHUNK if kind == "sb" else FOX_CHUNK
    padded = ((q_rows - N_META + length + chunk - 1) // chunk) * chunk
    ns = PROMPT_STREAMS
    wide = ns * LANES
    tok = pl.BlockSpec((1, length, wide), lambda i, p: (i, 0, p))
    feat = pl.BlockSpec((1, wide, length), lambda i, p: (i, p, 0))
    out_spec = pl.BlockSpec((1, seq, wide), lambda i, p: (i, 0, p))
    stage = pltpu.VMEM((LANES, padded), F32)
    if kind == "sb":
        body = functools.partial(_sb_prompt_body, n_qblocks=n_qblocks, n_streams=ns)
        in_specs = [tok, feat, feat, tok, pl.BlockSpec(extra.shape, lambda i, p: (0, 0))]
        args = (q, kt, vt, g, extra)
        scratch = [pltpu.VMEM((wide, padded), BF16), pltpu.VMEM((wide, padded), BF16), stage]
    else:
        body = functools.partial(_fox_prompt_body, n_qblocks=n_qblocks, n_streams=ns)
        in_specs = [tok, feat, feat, pl.BlockSpec((1, BIAS_ROWS, length), lambda i, p: (i, 0, 0)), tok]
        args = (q, kt, vt, extra, g)
        scratch = [pltpu.VMEM((2 * wide, padded), BF16), pltpu.VMEM((wide, padded), BF16), stage]
    return pl.pallas_call(
        body,
        grid=(b, WIDTH // wide),
        in_specs=in_specs,
        out_specs=out_spec,
        out_shape=jax.ShapeDtypeStruct((b, seq, WIDTH), BF16),
        scratch_shapes=scratch,
        compiler_params=pltpu.CompilerParams(dimension_semantics=("arbitrary", "arbitrary"),
                                             vmem_limit_bytes=VMEM_LIMIT),
        name=kind + "_prompt",
    )(*args)


def _sample_setup(q_ref, qv_s, dec):
    q = q_ref[0]
    lane_head = lax.broadcasted_iota(jnp.int32, q.shape, 1) >> 6
    for h in range(HEADS):
        qv_s[h * dec:(h + 1) * dec, :] = jnp.where(lane_head == h, q, jnp.zeros((), BF16))


def _pad_rows_bf16(x, rows):
    x = x.astype(BF16)
    return jnp.concatenate([x, jnp.zeros((rows - x.shape[0], x.shape[1]), BF16)], axis=0)


def _gather_heads(acc, dec):
    lane_head = lax.broadcasted_iota(jnp.int32, (dec, WIDTH), 1) >> 6
    out = jnp.zeros((dec, WIDTH), F32)
    for h in range(HEADS):
        out = jnp.where(lane_head == h, acc[h * dec:(h + 1) * dec, :], out)
    return out


def _sb_sample_body(q_ref, kn_ref, vn_ref, g_ref, ckt_ref, cvt_ref, tt_ref, o_ref,
                    qv_s, carry_s, acc_s, *, dec):
    j = pl.program_id(1)
    rows = HEADS * dec
    tt = tt_ref[...]

    @pl.when(j == 0)
    def _():
        _sample_setup(q_ref, qv_s, dec)
        row = lax.broadcasted_iota(jnp.int32, (rows, SUB), 0)
        col = lax.broadcasted_iota(jnp.int32, (rows, SUB), 1)
        carry, acc = _sb_rows(qv_s[...], _pad_rows_bf16(kn_ref[0], SUB), _pad_rows_bf16(vn_ref[0], SUB),
                              tt, jnp.zeros((rows, SUB), F32), jnp.zeros((rows, WIDTH), F32),
                              col < (row & (dec - 1)))
        carry_s[...] = carry
        acc_s[...] = acc

    @pl.when(jnp.min(carry_s[...]) < SB_DEAD_CARRY)
    def _():
        carry, acc = _sb_chunk(qv_s[...], ckt_ref[0].astype(BF16), cvt_ref[0].astype(BF16), tt,
                               carry_s[...], acc_s[...], None)
        carry_s[...] = carry
        acc_s[...] = acc

    @pl.when(j == pl.num_programs(1) - 1)
    def _():
        o_ref[0] = (_gather_heads(acc_s[...], dec) * _silu(g_ref[0])).astype(o_ref.dtype)


def _fox_sample_body(q_ref, kn_ref, vn_ref, csn_ref, g_ref, ckt_ref, cvt_ref, bh_ref, bm_ref, bl_ref,
                     o_ref, qv_s, m_s, l_s, acc_s, *, dec):
    j = pl.program_id(1)
    rows = HEADS * dec
    head = lax.broadcasted_iota(jnp.int32, (rows, 1), 0) >> (dec.bit_length() - 1)

    @pl.when(j == 0)
    def _():
        _sample_setup(q_ref, qv_s, dec)
        row = lax.broadcasted_iota(jnp.int32, (rows, SUB), 0)
        col = lax.broadcasted_iota(jnp.int32, (rows, SUB), 1)
        vn = _pad_rows_bf16(vn_ref[0], SUB)
        s = (_dot_nt(qv_s[...], _pad_rows_bf16(kn_ref[0], SUB))
             + _dot_nt(_one_hot_bias_cols(head, (rows, LANES)), _pad_rows_bf16(csn_ref[0], SUB)))
        m, l, acc = _softmax_step(s, jnp.full((rows, 1), NEG_BIG, F32), jnp.zeros((rows, 1), F32),
                                  jnp.zeros((rows, WIDTH), F32), col <= (row & (dec - 1)),
                                  lambda p: _dot(p, vn))
        m_s[...] = m
        l_s[...] = l
        acc_s[...] = acc

    tkb = ckt_ref.shape[2]
    bias = jnp.concatenate([bh_ref[0], bm_ref[0], bl_ref[0], jnp.zeros((HEADS, tkb), F32)],
                           axis=0).astype(BF16)
    vt = cvt_ref[0].astype(BF16)
    s = _dot(qv_s[...], ckt_ref[0].astype(BF16)) + _dot(_one_hot_bias_cols(head, (rows, BIAS_ROWS)), bias)
    m, l, acc = _softmax_step(s, m_s[...], l_s[...], acc_s[...], None, lambda p: _dot_nt(p, vt))
    m_s[...] = m
    l_s[...] = l
    acc_s[...] = acc

    @pl.when(j == pl.num_programs(1) - 1)
    def _():
        o_ref[0] = (_gather_heads(acc / l, dec) * _silu(g_ref[0])).astype(o_ref.dtype)


def _sample_attention(kind, q, kn, vn, g, cache_kt, cache_vt, extra, tkb):
    b, dec, _ = q.shape
    past = cache_kt.shape[2]
    nkb = past // tkb
    rows = HEADS * dec
    new = lambda w: pl.BlockSpec((1, dec, w), lambda i, j: (i, 0, 0))
    old = lambda r: pl.BlockSpec((1, r, tkb), lambda i, j: (i, 0, nkb - 1 - j))
    if kind == "sb":
        body = functools.partial(_sb_sample_body, dec=dec)
        in_specs = [new(WIDTH), new(WIDTH), new(WIDTH), new(WIDTH), old(WIDTH), old(WIDTH),
                    pl.BlockSpec(extra.shape, lambda i, j: (0, 0))]
        args = (q, kn, vn, g, cache_kt, cache_vt, extra)
        scratch = [pltpu.VMEM((rows, WIDTH), BF16), pltpu.VMEM((rows, SUB), F32),
                   pltpu.VMEM((rows, WIDTH), F32)]
    else:
        csn, bh, bm, bl = extra
        body = functools.partial(_fox_sample_body, dec=dec)
        in_specs = [new(WIDTH), new(WIDTH), new(WIDTH), new(LANES), new(WIDTH), old(WIDTH), old(WIDTH),
                    old(HEADS), old(HEADS), old(HEADS)]
        args = (q, kn, vn, csn, g, cache_kt, cache_vt, bh, bm, bl)
        scratch = [pltpu.VMEM((rows, WIDTH), BF16), pltpu.VMEM((rows, 1), F32),
                   pltpu.VMEM((rows, 1), F32), pltpu.VMEM((rows, WIDTH), F32)]
    return pl.pallas_call(
        body,
        grid=(b, nkb),
        in_specs=in_specs,
        out_specs=new(WIDTH),
        out_shape=jax.ShapeDtypeStruct((b, dec, WIDTH), BF16),
        scratch_shapes=scratch,
        compiler_params=pltpu.CompilerParams(dimension_semantics=("arbitrary", "arbitrary"),
                                             vmem_limit_bytes=VMEM_LIMIT),
        name=kind + "_sample",
    )(*args)


def _out_body(ma_ref, mb_ref, x_ref, wa_ref, wb_ref, g_ref, y_ref):
    h = x_ref[...] + _dot(ma_ref[...], wa_ref[...]) + _dot(mb_ref[...], wb_ref[...])
    ms = jnp.mean(h * h, axis=-1, keepdims=True)
    y_ref[...] = (h * lax.rsqrt(ms + EPS)) * g_ref[...]


def _output(ma, mb, x, wa, wb, g, tm):
    r = x.shape[0]
    row = lambda w: pl.BlockSpec((tm, w), lambda i: (i, 0))
    const = lambda s: pl.BlockSpec(s, lambda i: (0, 0))
    return pl.pallas_call(
        _out_body,
        grid=(r // tm,),
        in_specs=[row(WIDTH), row(WIDTH), row(D_MODEL), const(wa.shape), const(wb.shape),
                  const((1, D_MODEL))],
        out_specs=row(D_MODEL),
        out_shape=jax.ShapeDtypeStruct((r, D_MODEL), F32),
        compiler_params=pltpu.CompilerParams(dimension_semantics=("arbitrary",),
                                             vmem_limit_bytes=VMEM_LIMIT),
        name="output",
    )(ma, mb, x, wa, wb, g)


def _tri(n, cmp, group=None):
    i = np.arange(n)
    t = cmp(i[:, None], i[None, :])
    if group is not None:
        t &= (i[:, None] // group) == (i[None, :] // group)
    return t.astype(np.float32)


def _suffix_sum_matrix():
    t = _tri(SUB, np.greater_equal)
    half = np.concatenate([t, np.ones_like(t)], axis=1)
    return jnp.asarray(np.concatenate([half, half], axis=0), dtype=BF16)


def kernel(x_prompt, x_sample, cache_a_k, cache_a_v, cache_b_k, cache_b_v, cache_b_logf,
           meta_tokens, norm_g, w_in, b_f, w_out, final_g):
    depth = norm_g.shape[0]
    assert depth == 1, "single-layer trunk only"
    batch, seq, _ = x_prompt.shape
    dec_batch, dec_seq, _ = x_sample.shape
    past = cache_a_k.shape[2]
    length = N_META + seq
    tm = 256

    w = w_in[0]
    wt_all = jnp.transpose(w)
    slab = lambda a, c: a[c * WIDTH:(c + 1) * WIDTH]
    wm = w[:, :8 * WIDTH].astype(BF16)
    wrow = jnp.concatenate([w[:, c * WIDTH:(c + 1) * WIDTH] for c in (0, 3, 4, 7)], axis=1).astype(BF16)
    wt = jnp.concatenate([slab(wt_all, c) for c in (1, 2, 5, 6)], axis=0).astype(BF16)
    wft = jnp.pad(wt_all[8 * WIDTH:], ((0, 16 - HEADS), (0, 0))).astype(BF16)
    bft = jnp.broadcast_to(jnp.pad(b_f[0], (0, 16 - HEADS))[:, None], (16, LANES)).astype(F32)
    wf = jnp.pad(jnp.tile(w[:, 8 * WIDTH:], (1, 3)), ((0, 0), (0, LANES - 3 * HEADS))).astype(BF16)
    bf = jnp.pad(jnp.tile(b_f[0], 3), (0, LANES - 3 * HEADS)).reshape(1, LANES).astype(F32)
    g_in = norm_g[0].reshape(1, D_MODEL)
    g_out = final_g.reshape(1, D_MODEL)
    wa = w_out[0, :WIDTH].astype(BF16)
    wb = w_out[0, WIDTH:].astype(BF16)
    tt = _suffix_sum_matrix()
    triu = jnp.asarray(_tri(tm, np.less_equal), dtype=BF16)
    tri_s = jnp.asarray(_tri(tm, np.greater_equal, group=dec_seq), dtype=BF16)
    suffix_w = jnp.asarray(np.concatenate([_tri(tm, np.greater), np.ones((tm, LANES), np.float32)],
                                          axis=1), dtype=BF16)

    rows_in = -(-length // tm) * tm
    meta = jnp.broadcast_to(meta_tokens[None].astype(x_prompt.dtype), (batch, N_META, D_MODEL))
    hp = jnp.concatenate([meta, x_prompt, jnp.zeros((batch, rows_in - length, D_MODEL), x_prompt.dtype)],
                         axis=1)
    (qa, ga, qb, gb, kat, vat, kbt, vbt, lft, cst) = _project_prompt(
        hp, length, g_in, wrow, wt, wft, bft, triu, tm)
    mixed_a = _prompt_attention("sb", qa, kat, vat, ga, tt, seq)
    mixed_b = _prompt_attention("fox", qb, kbt, vbt, gb, cst, seq)
    y_prompt = _output(mixed_a.reshape(batch * seq, WIDTH), mixed_b.reshape(batch * seq, WIDTH),
                       x_prompt.reshape(batch * seq, D_MODEL), wa, wb, g_out, 2 * tm).reshape(batch, seq, D_MODEL)

    s_rows = dec_batch * dec_seq
    (sqa, ska, sva, sga, sqb, skb, svb, sgb, slf, scs) = _project_sample(
        x_sample.reshape(s_rows, D_MODEL), g_in, wm, wf, bf, tri_s, tm)
    clft = jnp.transpose(cache_b_logf[0], (0, 2, 1)).reshape(dec_batch * HEADS, past)
    bias_parts = [p.reshape(dec_batch, HEADS, past) for p in _cache_bias(clft, suffix_w, tm)]
    per_b = lambda a: a.reshape(dec_batch, dec_seq, a.shape[-1])
    feat_major = lambda c: jnp.transpose(c[0], (0, 2, 3, 1)).reshape(dec_batch, WIDTH, past)
    smix_a = _sample_attention("sb", per_b(sqa), per_b(ska), per_b(sva), per_b(sga),
                               feat_major(cache_a_k), feat_major(cache_a_v), tt, SAMPLE_CHUNK)
    smix_b = _sample_attention("fox", per_b(sqb), per_b(skb), per_b(svb), per_b(sgb),
                               feat_major(cache_b_k), feat_major(cache_b_v), (per_b(scs), *bias_parts),
                               SAMPLE_CHUNK)
    y_sample = _output(smix_a.reshape(s_rows, WIDTH), smix_b.reshape(s_rows, WIDTH),
                       x_sample.reshape(s_rows, D_MODEL), wa, wb, g_out, 2 * tm).reshape(dec_batch, dec_seq, D_MODEL)

    def prompt_heads(at):
        return jnp.transpose(at.reshape(batch, HEADS, HEAD_DIM, length), (0, 3, 1, 2))[None]

    sample_heads = lambda a: a.reshape(1, dec_batch, dec_seq, HEADS, HEAD_DIM)
    return (y_prompt, y_sample,
            prompt_heads(kat), prompt_heads(vat), prompt_heads(kbt), prompt_heads(vbt),
            jnp.transpose(lft, (0, 2, 1))[None],
            sample_heads(ska), sample_heads(sva), sample_heads(skb), sample_heads(svb),
            slf.reshape(1, dec_batch, dec_seq, HEADS))
```

```python
import functools

import jax
import jax.numpy as jnp
import numpy as np
from jax import lax
from jax.experimental import pallas as pl
from jax.experimental.pallas import tpu as pltpu

F32 = jnp.float32
BF16 = jnp.bfloat16

D_MODEL = 1024
HEAD_DIM = 64
HEADS = 8
WIDTH = HEADS * HEAD_DIM
N_META = 16
EPS = 1e-6
LANES = 128
SUB = 128
CHUNK = 512
FOX_CHUNK = 1024
SAMPLE_CHUNK = 1024
Q_ROWS = 128
FOX_Q_ROWS = 256
LOG2_E = 1.4426950408889634
SB_DEAD_CARRY = 104.0
BIAS_ROWS = 32
PAD_KEYS = SUB - N_META
PROMPT_STREAMS = 2
NEG_BIG = -1e30
VMEM_LIMIT = 56 * 1024 * 1024


def _split3(x):
    hi = x.astype(BF16)
    r1 = x - hi.astype(F32)
    mid = r1.astype(BF16)
    lo = (r1 - mid.astype(F32)).astype(BF16)
    return hi, mid, lo


def _log_sigmoid(x):
    return jnp.minimum(x, 0.0) - jnp.log(1.0 + jnp.exp(-jnp.abs(x)))


def _silu(g):
    return g / (1.0 + jnp.exp(-g))


def _dot(a, b):
    return jnp.dot(a, b, preferred_element_type=F32)


def _dot_nt(a, b):
    return lax.dot_general(a, b, (((1,), (1,)), ((), ())), preferred_element_type=F32)


def _rms_normed(x, g):
    ms = jnp.mean(x * x, axis=-1, keepdims=True)
    return ((x * lax.rsqrt(ms + EPS)) * g).astype(BF16)


def _cache_bias_body(x_ref, w_ref, hi_ref, mid_ref, lo_ref, carry_ref):
    @pl.when(pl.program_id(0) == 0)
    def _():
        carry_ref[...] = jnp.zeros_like(carry_ref)

    x = x_ref[...]
    tb = x.shape[1]
    hi, mid, lo = _split3(x)
    w = w_ref[...]
    r = _dot(hi, w) + _dot(mid, w) + _dot(lo, w)
    carry = carry_ref[...]
    s = r[:, :tb] + jnp.concatenate([carry] * (tb // LANES), axis=1)
    carry_ref[...] = carry + r[:, tb:]
    h, m, l = _split3(s)
    hi_ref[...] = h.astype(F32)
    mid_ref[...] = m.astype(F32)
    lo_ref[...] = l.astype(F32)


def _cache_bias(x, w, tb):
    rows, past = x.shape
    nb = past // tb
    blk = pl.BlockSpec((rows, tb), lambda t: (0, nb - 1 - t))
    return pl.pallas_call(
        _cache_bias_body,
        grid=(nb,),
        in_specs=[blk, pl.BlockSpec(w.shape, lambda t: (0, 0))],
        out_specs=[blk, blk, blk],
        out_shape=[jax.ShapeDtypeStruct((rows, past), F32)] * 3,
        scratch_shapes=[pltpu.VMEM((rows, LANES), F32)],
        compiler_params=pltpu.CompilerParams(dimension_semantics=("arbitrary",),
                                             vmem_limit_bytes=VMEM_LIMIT),
        name="cache_bias",
    )(x, w)


def _proj_prompt_body(x_ref, g_ref, wt_ref, wft_ref, bft_ref, triu_ref,
                      qa_ref, ga_ref, qb_ref, gb_ref, kat_ref, vat_ref, kbt_ref, vbt_ref,
                      lft_ref, cst_ref, carry_ref):
    t = pl.program_id(1)
    xb = _rms_normed(x_ref[0], g_ref[...])
    tm = xb.shape[0]
    slab = lambda c: wt_ref[c * WIDTH:(c + 1) * WIDTH, :]
    for c, o_ref in ((0, qa_ref), (3, ga_ref), (4, qb_ref), (7, gb_ref)):
        u = _dot_nt(xb, slab(c))
        if c % 4 == 0:
            u = u * (1.0 / 8.0)
        o_ref[0] = u.astype(o_ref.dtype)
    for c, o_ref in ((1, kat_ref), (2, vat_ref), (5, kbt_ref), (6, vbt_ref)):
        o_ref[0] = _dot_nt(slab(c), xb)

    reps = tm // LANES
    flt = _dot_nt(wft_ref[...], xb) + jnp.concatenate([bft_ref[...]] * reps, axis=1)
    logf = _log_sigmoid(flt)
    lft_ref[0] = logf[0:HEADS]
    hi, mid, lo = _split3(logf)
    triu = triu_ref[...]
    c = _dot(hi, triu) + _dot(mid, triu) + _dot(lo, triu)

    @pl.when(t == 0)
    def _():
        carry_ref[...] = jnp.zeros_like(carry_ref)

    c = c + jnp.concatenate([carry_ref[...]] * reps, axis=1)
    carry_ref[...] = jnp.broadcast_to(c[:, tm - 1:tm], carry_ref.shape)
    h, m, l = _split3(-c[0:HEADS])
    cst_ref[0] = jnp.concatenate([h.astype(F32), m.astype(F32), l.astype(F32),
                                  jnp.zeros((HEADS, tm), F32)], axis=0).astype(BF16)


def _project_prompt(hp, length, g, wt, wft, bft, triu, tm):
    b, rows_in, _ = hp.shape
    nt = rows_in // tm
    const = lambda s: pl.BlockSpec(s, lambda i, t: (0,) * len(s))
    tok = lambda w: pl.BlockSpec((1, tm, w), lambda i, t: (i, t, 0))
    feat = lambda r: pl.BlockSpec((1, r, tm), lambda i, t: (i, 0, t))
    out_shape = ([jax.ShapeDtypeStruct((b, length, WIDTH), dt) for dt in (BF16, F32, BF16, F32)]
                 + [jax.ShapeDtypeStruct((b, WIDTH, length), F32)] * 4
                 + [jax.ShapeDtypeStruct((b, HEADS, length), F32),
                    jax.ShapeDtypeStruct((b, BIAS_ROWS, length), BF16)])
    return pl.pallas_call(
        _proj_prompt_body,
        grid=(b, nt),
        in_specs=[tok(D_MODEL), const((1, D_MODEL)), const(wt.shape),
                  const(wft.shape), const(bft.shape), const(triu.shape)],
        out_specs=[tok(WIDTH)] * 4 + [feat(WIDTH)] * 4 + [feat(HEADS), feat(BIAS_ROWS)],
        out_shape=out_shape,
        scratch_shapes=[pltpu.VMEM((16, LANES), F32)],
        compiler_params=pltpu.CompilerParams(dimension_semantics=("arbitrary", "arbitrary"),
                                             vmem_limit_bytes=VMEM_LIMIT),
        name="project_prompt",
    )(hp, g, wt, wft, bft, triu)


def _proj_sample_body(x_ref, g_ref, wt_ref, wf_ref, bf_ref, tri_ref,
                      qa_ref, ka_ref, va_ref, ga_ref, qb_ref, kb_ref, vb_ref, gb_ref, lf_ref, cs_ref):
    xb = _rms_normed(x_ref[...], g_ref[...])
    tm = xb.shape[0]
    for c, o_ref in enumerate((qa_ref, ka_ref, va_ref, ga_ref, qb_ref, kb_ref, vb_ref, gb_ref)):
        u = _dot_nt(xb, wt_ref[c * WIDTH:(c + 1) * WIDTH, :])
        if c % 4 == 0:
            u = u * (1.0 / 8.0)
        o_ref[...] = u.astype(o_ref.dtype)
    logf = _log_sigmoid(_dot_nt(xb, wf_ref[...]) + bf_ref[...])
    lf_ref[...] = logf[:, 0:HEADS]
    lane = lax.broadcasted_iota(jnp.int32, (tm, LANES), 1)
    hi, mid, lo = _split3(jnp.where(lane < 3 * HEADS, logf, 0.0))
    tri = tri_ref[...]
    c = _dot(tri, hi) + _dot(tri, mid) + _dot(tri, lo)
    h, m, l = _split3(-c)
    zero = jnp.zeros_like(h)
    cs_ref[...] = jnp.where(lane < 8, h, jnp.where(lane < 16, m, jnp.where(lane < 24, l, zero)))


def _project_sample(x, g, wm, wf, bf, tri, tm):
    rows = x.shape[0]
    row = lambda w: pl.BlockSpec((tm, w), lambda t: (t, 0))
    const = lambda s: pl.BlockSpec(s, lambda t: (0,) * len(s))
    slab_dtypes = [BF16, F32, F32, F32, BF16, F32, F32, F32]
    out_shape = ([jax.ShapeDtypeStruct((rows, WIDTH), dt) for dt in slab_dtypes]
                 + [jax.ShapeDtypeStruct((rows, HEADS), F32), jax.ShapeDtypeStruct((rows, LANES), BF16)])
    return pl.pallas_call(
        _proj_sample_body,
        grid=(rows // tm,),
        in_specs=[row(D_MODEL), const((1, D_MODEL)), const(wm.shape), const(wf.shape),
                  const((1, LANES)), const((tm, tm))],
        out_specs=[row(WIDTH)] * 8 + [row(HEADS), row(LANES)],
        out_shape=out_shape,
        compiler_params=pltpu.CompilerParams(dimension_semantics=("arbitrary",),
                                             vmem_limit_bytes=VMEM_LIMIT),
        name="project_sample",
    )(x, g, wm, wf, bf, tri)


def _sb_suffix(z, sp, tt, carry, valid):
    hi = sp.astype(BF16)
    lo = (sp - hi.astype(F32)).astype(BF16)
    cs = _dot(jnp.concatenate([hi, lo], axis=1), tt)
    w = jnp.exp(z - cs[:, :SUB] - carry)
    if valid is not None:
        w = jnp.where(valid, w, 0.0)
    return w.astype(BF16), carry + cs[:, SUB:]


def _softplus(z, valid):
    sp = jnp.maximum(z, 0.0) + jnp.log(1.0 + jnp.exp2(jnp.abs(z) * (-LOG2_E)))
    if valid is not None:
        sp = jnp.where(valid, sp, 0.0)
    return sp


def _sb_chunk(qv, kt, vt, tt, carry, acc, newest_valid):
    z = _dot(qv, kt)
    n = z.shape[1] // SUB
    ws = []
    for u in reversed(range(n)):
        zu = z[:, u * SUB:(u + 1) * SUB]
        valid = newest_valid if u == n - 1 else None
        w, carry = _sb_suffix(zu, _softplus(zu, valid), tt, carry, valid)
        ws.append(w)
    acc = acc + _dot_nt(jnp.concatenate(ws[::-1], axis=1), vt)
    return carry, acc


def _sb_rows(qv, k, v, tt, carry, acc, valid):
    z = _dot_nt(qv, k)
    w, carry = _sb_suffix(z, _softplus(z, valid), tt, carry, valid)
    return carry, acc + _dot(w, v)


def _softmax_step(s, m, l, acc, valid, pv):
    if valid is not None:
        s = jnp.where(valid, s, NEG_BIG)
    m_new = jnp.maximum(m, jnp.max(s, axis=-1, keepdims=True))
    alpha = jnp.exp(m - m_new)
    p = jnp.exp(s - m_new)
    l = alpha * l + jnp.sum(p, axis=-1, keepdims=True)
    acc = alpha * acc + pv(p.astype(BF16))
    return m_new, l, acc


def _one_hot_bias_cols(head, shape):
    col = lax.broadcasted_iota(jnp.int32, shape, 1)
    hit = (col == head) | (col == head + HEADS) | (col == head + 2 * HEADS)
    return jnp.where(hit, 1.0, 0.0).astype(BF16)


def _shifted(src, stage, pad):
    r, length = src.shape
    tail = (length // LANES) * LANES
    stage[0:r, tail:] = jnp.zeros((r, stage.shape[1] - tail), F32)
    stage[0:r, 0:length] = src
    return pltpu.roll(stage[0:r, :], pad, axis=1)


def _prompt_geometry(chunk, q_rows):
    rows = 2 * q_rows
    row = lax.broadcasted_iota(jnp.int32, (rows, chunk), 0)
    col = lax.broadcasted_iota(jnp.int32, (rows, chunk), 1)
    qidx = row & (q_rows - 1)
    r128 = lax.broadcasted_iota(jnp.int32, (rows, LANES), 0)
    c128 = lax.broadcasted_iota(jnp.int32, (rows, LANES), 1)
    head_mask = (r128 < q_rows) == (c128 < HEAD_DIM)
    first_head_lanes = lax.broadcasted_iota(jnp.int32, (q_rows, LANES), 1) < HEAD_DIM
    return col, qidx, head_mask, first_head_lanes


def _prompt_q_loop(q_rows, n_qblocks, n_streams, q_ref, g_ref, o_ref, head_mask, first_head_lanes, attend,
                   period=1):
    def q_block(qi, phase):
        qs = pl.multiple_of(N_META + qi * q_rows, 16)
        q = q_ref[0, pl.ds(qs, q_rows), :]
        qvs = []
        for s in range(n_streams):
            qp = q[:, s * LANES:(s + 1) * LANES]
            qvs.append(jnp.where(head_mask, jnp.concatenate([qp, qp], axis=0), jnp.zeros((), BF16)))
        outs = [jnp.where(first_head_lanes, o[:q_rows], o[q_rows:]) for o in attend(qi, qvs, phase)]
        g = g_ref[0, pl.ds(qs, q_rows), :]
        os_ = pl.multiple_of(qi * q_rows, q_rows)
        o_ref[0, pl.ds(os_, q_rows), :] = (jnp.concatenate(outs, axis=1) * _silu(g)).astype(o_ref.dtype)

    def group(gi, _):
        for phase in range(period):
            q_block(gi * period + phase, phase)
        return 0

    lax.fori_loop(0, n_qblocks // period, group, 0)


def _chunk_plan(qi, chunk):
    blk = qi + 1
    per = chunk // SUB
    return blk // per, (blk % per) * SUB


def _chunk_ref(ref, r0, nrows, c, chunk):
    return ref[r0:r0 + nrows, pl.ds(pl.multiple_of(c * chunk, chunk), chunk)]


def _sb_prompt_body(q_ref, kt_ref, vt_ref, g_ref, tt_ref, o_ref, k16, v16, stage, *, n_qblocks, n_streams):
    for s in range(n_streams):
        rs = slice(s * LANES, (s + 1) * LANES)
        k16[rs, :] = _shifted(kt_ref[0, rs, :], stage, PAD_KEYS).astype(BF16)
        v16[rs, :] = _shifted(vt_ref[0, rs, :], stage, PAD_KEYS).astype(BF16)
    col, qidx, head_mask, first_head_lanes = _prompt_geometry(SUB, Q_ROWS)
    diag_valid = col < qidx
    tt = tt_ref[...]
    rows = 2 * Q_ROWS
    per = CHUNK // SUB

    def attend(qi, qvs, phase):
        own = (phase + 1) % per
        last = (qi + 1) // per
        first_width = (own + 1) * SUB

        def chunk(c, state, width, newest_valid):
            lanes = pl.ds(pl.multiple_of(c * CHUNK, CHUNK), width)
            return tuple(
                _sb_chunk(qvs[s], k16[s * LANES:(s + 1) * LANES, lanes], v16[s * LANES:(s + 1) * LANES, lanes],
                          tt, carry, acc, newest_valid)
                for s, (carry, acc) in enumerate(state))

        def live(state):
            low = state[0][0]
            for carry, _ in state[1:]:
                low = jnp.minimum(low, carry)
            return jnp.min(low) < SB_DEAD_CARRY

        zero = jnp.zeros((rows, LANES), F32)
        state = chunk(last, ((zero, zero),) * n_streams, first_width, diag_valid)

        def older(loop):
            j, _, state = loop
            state = chunk(last - 1 - j, state, CHUNK, None)
            return j + 1, live(state), state

        _, _, state = lax.while_loop(lambda loop: (loop[0] < last) & loop[1], older,
                                     (jnp.int32(0), live(state), state))
        return [acc for _, acc in state]

    _prompt_q_loop(Q_ROWS, n_qblocks, n_streams, q_ref, g_ref, o_ref, head_mask, first_head_lanes, attend,
                   period=per)


def _fox_prompt_body(q_ref, kt_ref, vt_ref, cst_ref, g_ref, o_ref, kc16, v16, stage, *, n_qblocks, n_streams):
    q_rows = FOX_Q_ROWS
    pad = q_rows - N_META
    bias = _shifted(cst_ref[0].astype(F32), stage, pad)
    brow = lax.broadcasted_iota(jnp.int32, bias.shape, 0)
    blane = lax.broadcasted_iota(jnp.int32, bias.shape, 1)
    bias = jnp.where((blane < pad) & (brow < HEADS), NEG_BIG, bias).astype(BF16)
    kc_rows = 2 * LANES
    for s in range(n_streams):
        rs = slice(s * LANES, (s + 1) * LANES)
        kc16[s * kc_rows:s * kc_rows + LANES, :] = _shifted(kt_ref[0, rs, :], stage, pad).astype(BF16)
        kc16[s * kc_rows + LANES:s * kc_rows + LANES + BIAS_ROWS, :] = bias
        kc16[s * kc_rows + LANES + BIAS_ROWS:(s + 1) * kc_rows, :] = jnp.zeros(
            (LANES - BIAS_ROWS, kc16.shape[1]), BF16)
        v16[rs, :] = _shifted(vt_ref[0, rs, :], stage, pad).astype(BF16)
    col, qidx, head_mask, first_head_lanes = _prompt_geometry(FOX_CHUNK, q_rows)
    rows = 2 * q_rows
    r128 = lax.broadcasted_iota(jnp.int32, (rows, LANES), 0)
    first_head = 2 * n_streams * pl.program_id(1) + jnp.where(r128 < q_rows, 0, 1)
    qones = [_one_hot_bias_cols(first_head + 2 * s, (rows, LANES)) for s in range(n_streams)]

    def attend(qi, qvs, phase):
        tile_lane = (qi + 1) * q_rows
        last = tile_lane // FOX_CHUNK
        qv2s = [jnp.concatenate([qv, qone], axis=1) for qv, qone in zip(qvs, qones)]

        def chunk(c, state, valid):
            out = []
            for s, (m, l, acc) in enumerate(state):
                vt = _chunk_ref(v16, s * LANES, LANES, c, FOX_CHUNK)
                sc = _dot(qv2s[s], _chunk_ref(kc16, s * kc_rows, kc_rows, c, FOX_CHUNK))
                out.append(_softmax_step(sc, m, l, acc, valid, lambda p, vt=vt: _dot_nt(p, vt)))
            return tuple(out)

        init = (jnp.full((rows, 1), NEG_BIG, F32), jnp.zeros((rows, 1), F32), jnp.zeros((rows, LANES), F32))
        state = chunk(last, (init,) * n_streams, col + (last * FOX_CHUNK - tile_lane) <= qidx)
        state = lax.fori_loop(0, last, lambda j, st: chunk(last - 1 - j, st, None), state)
        return [acc / l for _, l, acc in state]

    _prompt_q_loop(q_rows, n_qblocks, n_streams, q_ref, g_ref, o_ref, head_mask, first_head_lanes, attend)


def _prompt_attention(kind, q, kt, vt, g, extra, seq):
    b, length, _ = q.shape
    q_rows = Q_ROWS if kind == "sb" else FOX_Q_ROWS
    n_qblocks = seq // q_rows
    chunk = CHUNK if kind == "sb" else FOX_CHUNK
    padded = ((q_rows - N_META + length + chunk - 1) // chunk) * chunk
    ns = PROMPT_STREAMS
    wide = ns * LANES
    tok = pl.BlockSpec((1, length, wide), lambda i, p: (i, 0, p))
    feat = pl.BlockSpec((1, wide, length), lambda i, p: (i, p, 0))
    out_spec = pl.BlockSpec((1, seq, wide), lambda i, p: (i, 0, p))
    stage = pltpu.VMEM((LANES, padded), F32)
    if kind == "sb":
        body = functools.partial(_sb_prompt_body, n_qblocks=n_qblocks, n_streams=ns)
        in_specs = [tok, feat, feat, tok, pl.BlockSpec(extra.shape, lambda i, p: (0, 0))]
        args = (q, kt, vt, g, extra)
        scratch = [pltpu.VMEM((wide, padded), BF16), pltpu.VMEM((wide, padded), BF16), stage]
    else:
        body = functools.partial(_fox_prompt_body, n_qblocks=n_qblocks, n_streams=ns)
        in_specs = [tok, feat, feat, pl.BlockSpec((1, BIAS_ROWS, length), lambda i, p: (i, 0, 0)), tok]
        args = (q, kt, vt, extra, g)
        scratch = [pltpu.VMEM((2 * wide, padded), BF16), pltpu.VMEM((wide, padded), BF16), stage]
    return pl.pallas_call(
        body,
        grid=(b, WIDTH // wide),
        in_specs=in_specs,
        out_specs=out_spec,
        out_shape=jax.ShapeDtypeStruct((b, seq, WIDTH), BF16),
        scratch_shapes=scratch,
        compiler_params=pltpu.CompilerParams(dimension_semantics=("arbitrary", "arbitrary"),
                                             vmem_limit_bytes=VMEM_LIMIT),
        name=kind + "_prompt",
    )(*args)


def _sample_setup(q_ref, qv_s, dec):
    q = q_ref[0]
    lane_head = lax.broadcasted_iota(jnp.int32, q.shape, 1) >> 6
    for h in range(HEADS):
        qv_s[h * dec:(h + 1) * dec, :] = jnp.where(lane_head == h, q, jnp.zeros((), BF16))


def _pad_rows_bf16(x, rows):
    x = x.astype(BF16)
    return jnp.concatenate([x, jnp.zeros((rows - x.shape[0], x.shape[1]), BF16)], axis=0)


def _gather_heads(acc, dec):
    lane_head = lax.broadcasted_iota(jnp.int32, (dec, WIDTH), 1) >> 6
    out = jnp.zeros((dec, WIDTH), F32)
    for h in range(HEADS):
        out = jnp.where(lane_head == h, acc[h * dec:(h + 1) * dec, :], out)
    return out


def _sb_sample_body(q_ref, kn_ref, vn_ref, g_ref, ckt_hbm, cvt_hbm, tt_ref, o_ref,
                    qv_s, kbuf, vbuf, sems, *, dec, tkb):
    b = pl.program_id(0)
    rows = HEADS * dec
    tt = tt_ref[...]
    nkb = ckt_hbm.shape[2] // tkb

    def block_copies(j):
        lanes = pl.ds(pl.multiple_of((nkb - 1 - j) * tkb, tkb), tkb)
        return (pltpu.make_async_copy(ckt_hbm.at[b, :, lanes], kbuf, sems.at[0]),
                pltpu.make_async_copy(cvt_hbm.at[b, :, lanes], vbuf, sems.at[1]))

    def start(j):
        for cp in block_copies(j):
            cp.start()

    start(0)
    _sample_setup(q_ref, qv_s, dec)
    qv = qv_s[...]
    row = lax.broadcasted_iota(jnp.int32, (rows, SUB), 0)
    col = lax.broadcasted_iota(jnp.int32, (rows, SUB), 1)
    carry, acc = _sb_rows(qv, _pad_rows_bf16(kn_ref[0], SUB), _pad_rows_bf16(vn_ref[0], SUB),
                          tt, jnp.zeros((rows, SUB), F32), jnp.zeros((rows, WIDTH), F32),
                          col < (row & (dec - 1)))

    def older(loop):
        j, carry, acc = loop
        for cp in block_copies(j):
            cp.wait()
        carry, acc = _sb_chunk(qv, kbuf[...].astype(BF16), vbuf[...].astype(BF16), tt, carry, acc, None)
        nxt = j + 1
        more = (nxt < nkb) & (jnp.min(carry) < SB_DEAD_CARRY)

        @pl.when(more)
        def _():
            start(nxt)

        return jnp.where(more, nxt, nkb), carry, acc

    _, _, acc = lax.while_loop(lambda loop: loop[0] < nkb, older, (jnp.int32(0), carry, acc))
    o_ref[0] = (_gather_heads(acc, dec) * _silu(g_ref[0])).astype(o_ref.dtype)


def _fox_sample_body(q_ref, kn_ref, vn_ref, csn_ref, g_ref, ckt_ref, cvt_ref, bh_ref, bm_ref, bl_ref,
                     o_ref, qv_s, m_s, l_s, acc_s, *, dec):
    j = pl.program_id(1)
    rows = HEADS * dec
    head = lax.broadcasted_iota(jnp.int32, (rows, 1), 0) >> (dec.bit_length() - 1)

    @pl.when(j == 0)
    def _():
        _sample_setup(q_ref, qv_s, dec)
        row = lax.broadcasted_iota(jnp.int32, (rows, SUB), 0)
        col = lax.broadcasted_iota(jnp.int32, (rows, SUB), 1)
        vn = _pad_rows_bf16(vn_ref[0], SUB)
        s = (_dot_nt(qv_s[...], _pad_rows_bf16(kn_ref[0], SUB))
             + _dot_nt(_one_hot_bias_cols(head, (rows, LANES)), _pad_rows_bf16(csn_ref[0], SUB)))
        m, l, acc = _softmax_step(s, jnp.full((rows, 1), NEG_BIG, F32), jnp.zeros((rows, 1), F32),
                                  jnp.zeros((rows, WIDTH), F32), col <= (row & (dec - 1)),
                                  lambda p: _dot(p, vn))
        m_s[...] = m
        l_s[...] = l
        acc_s[...] = acc

    tkb = ckt_ref.shape[2]
    bias = jnp.concatenate([bh_ref[0], bm_ref[0], bl_ref[0], jnp.zeros((HEADS, tkb), F32)],
                           axis=0).astype(BF16)
    vt = cvt_ref[0].astype(BF16)
    s = _dot(qv_s[...], ckt_ref[0].astype(BF16)) + _dot(_one_hot_bias_cols(head, (rows, BIAS_ROWS)), bias)
    m, l, acc = _softmax_step(s, m_s[...], l_s[...], acc_s[...], None, lambda p: _dot_nt(p, vt))
    m_s[...] = m
    l_s[...] = l
    acc_s[...] = acc

    @pl.when(j == pl.num_programs(1) - 1)
    def _():
        o_ref[0] = (_gather_heads(acc / l, dec) * _silu(g_ref[0])).astype(o_ref.dtype)


def _sample_attention(kind, q, kn, vn, g, cache_kt, cache_vt, extra, tkb):
    b, dec, _ = q.shape
    past = cache_kt.shape[2]
    nkb = past // tkb
    rows = HEADS * dec
    new = lambda w: pl.BlockSpec((1, dec, w), lambda i, j: (i, 0, 0))
    old = lambda r: pl.BlockSpec((1, r, tkb), lambda i, j: (i, 0, nkb - 1 - j))
    if kind == "sb":
        per_stream = lambda w: pl.BlockSpec((1, dec, w), lambda i: (i, 0, 0))
        in_hbm = pl.BlockSpec(memory_space=pl.ANY)
        return pl.pallas_call(
            functools.partial(_sb_sample_body, dec=dec, tkb=tkb),
            grid=(b,),
            in_specs=[per_stream(WIDTH)] * 4 + [in_hbm, in_hbm, pl.BlockSpec(extra.shape, lambda i: (0, 0))],
            out_specs=per_stream(WIDTH),
            out_shape=jax.ShapeDtypeStruct((b, dec, WIDTH), BF16),
            scratch_shapes=[pltpu.VMEM((rows, WIDTH), BF16), pltpu.VMEM((WIDTH, tkb), F32),
                            pltpu.VMEM((WIDTH, tkb), F32), pltpu.SemaphoreType.DMA((2,))],
            compiler_params=pltpu.CompilerParams(dimension_semantics=("arbitrary",),
                                                 vmem_limit_bytes=VMEM_LIMIT),
            name="sb_sample",
        )(q, kn, vn, g, cache_kt, cache_vt, extra)
    else:
        csn, bh, bm, bl = extra
        body = functools.partial(_fox_sample_body, dec=dec)
        in_specs = [new(WIDTH), new(WIDTH), new(WIDTH), new(LANES), new(WIDTH), old(WIDTH), old(WIDTH),
                    old(HEADS), old(HEADS), old(HEADS)]
        args = (q, kn, vn, csn, g, cache_kt, cache_vt, bh, bm, bl)
        scratch = [pltpu.VMEM((rows, WIDTH), BF16), pltpu.VMEM((rows, 1), F32),
                   pltpu.VMEM((rows, 1), F32), pltpu.VMEM((rows, WIDTH), F32)]
    return pl.pallas_call(
        body,
        grid=(b, nkb),
        in_specs=in_specs,
        out_specs=new(WIDTH),
        out_shape=jax.ShapeDtypeStruct((b, dec, WIDTH), BF16),
        scratch_shapes=scratch,
        compiler_params=pltpu.CompilerParams(dimension_semantics=("arbitrary", "arbitrary"),
                                             vmem_limit_bytes=VMEM_LIMIT),
        name=kind + "_sample",
    )(*args)


def _out_body(ma_ref, mb_ref, x_ref, wa_ref, wb_ref, g_ref, y_ref):
    h = x_ref[...] + _dot(ma_ref[...], wa_ref[...]) + _dot(mb_ref[...], wb_ref[...])
    ms = jnp.mean(h * h, axis=-1, keepdims=True)
    y_ref[...] = (h * lax.rsqrt(ms + EPS)) * g_ref[...]


def _output(ma, mb, x, wa, wb, g, tm):
    r = x.shape[0]
    row = lambda w: pl.BlockSpec((tm, w), lambda i: (i, 0))
    const = lambda s: pl.BlockSpec(s, lambda i: (0, 0))
    return pl.pallas_call(
        _out_body,
        grid=(r // tm,),
        in_specs=[row(WIDTH), row(WIDTH), row(D_MODEL), const(wa.shape), const(wb.shape),
                  const((1, D_MODEL))],
        out_specs=row(D_MODEL),
        out_shape=jax.ShapeDtypeStruct((r, D_MODEL), F32),
        compiler_params=pltpu.CompilerParams(dimension_semantics=("arbitrary",),
                                             vmem_limit_bytes=VMEM_LIMIT),
        name="output",
    )(ma, mb, x, wa, wb, g)


def _tri(n, cmp, group=None):
    i = np.arange(n)
    t = cmp(i[:, None], i[None, :])
    if group is not None:
        t &= (i[:, None] // group) == (i[None, :] // group)
    return t.astype(np.float32)


def _suffix_sum_matrix():
    t = _tri(SUB, np.greater_equal)
    half = np.concatenate([t, np.ones_like(t)], axis=1)
    return jnp.asarray(np.concatenate([half, half], axis=0), dtype=BF16)


def kernel(x_prompt, x_sample, cache_a_k, cache_a_v, cache_b_k, cache_b_v, cache_b_logf,
           meta_tokens, norm_g, w_in, b_f, w_out, final_g):
    depth = norm_g.shape[0]
    assert depth == 1, "single-layer trunk only"
    batch, seq, _ = x_prompt.shape
    dec_batch, dec_seq, _ = x_sample.shape
    past = cache_a_k.shape[2]
    length = N_META + seq
    tm = 256

    wt_all = jnp.transpose(w_in[0]).astype(BF16)
    wt = wt_all[:8 * WIDTH]
    wf_rows = wt_all[8 * WIDTH:]
    wft = jnp.pad(wf_rows, ((0, 16 - HEADS), (0, 0)))
    bft = jnp.broadcast_to(jnp.pad(b_f[0], (0, 16 - HEADS))[:, None], (16, LANES)).astype(F32)
    wf = jnp.pad(jnp.tile(wf_rows, (3, 1)), ((0, LANES - 3 * HEADS), (0, 0)))
    bf = jnp.pad(jnp.tile(b_f[0], 3), (0, LANES - 3 * HEADS)).reshape(1, LANES).astype(F32)
    g_in = norm_g[0].reshape(1, D_MODEL)
    g_out = final_g.reshape(1, D_MODEL)
    wa = w_out[0, :WIDTH].astype(BF16)
    wb = w_out[0, WIDTH:].astype(BF16)
    tt = _suffix_sum_matrix()
    triu = jnp.asarray(_tri(tm, np.less_equal), dtype=BF16)
    tri_s = jnp.asarray(_tri(tm, np.greater_equal, group=dec_seq), dtype=BF16)
    suffix_w = jnp.asarray(np.concatenate([_tri(tm, np.greater), np.ones((tm, LANES), np.float32)],
                                          axis=1), dtype=BF16)

    rows_in = -(-length // tm) * tm
    meta = jnp.broadcast_to(meta_tokens[None].astype(x_prompt.dtype), (batch, N_META, D_MODEL))
    hp = jnp.concatenate([meta, x_prompt, jnp.zeros((batch, rows_in - length, D_MODEL), x_prompt.dtype)],
                         axis=1)
    (qa, ga, qb, gb, kat, vat, kbt, vbt, lft, cst) = _project_prompt(
        hp, length, g_in, wt, wft, bft, triu, tm)
    mixed_a = _prompt_attention("sb", qa, kat, vat, ga, tt, seq)
    mixed_b = _prompt_attention("fox", qb, kbt, vbt, gb, cst, seq)
    y_prompt = _output(mixed_a.reshape(batch * seq, WIDTH), mixed_b.reshape(batch * seq, WIDTH),
                       x_prompt.reshape(batch * seq, D_MODEL), wa, wb, g_out, 2 * tm).reshape(batch, seq, D_MODEL)

    s_rows = dec_batch * dec_seq
    (sqa, ska, sva, sga, sqb, skb, svb, sgb, slf, scs) = _project_sample(
        x_sample.reshape(s_rows, D_MODEL), g_in, wt, wf, bf, tri_s, tm)
    clft = jnp.transpose(cache_b_logf[0], (0, 2, 1)).reshape(dec_batch * HEADS, past)
    bias_parts = [p.reshape(dec_batch, HEADS, past) for p in _cache_bias(clft, suffix_w, tm)]
    per_b = lambda a: a.reshape(dec_batch, dec_seq, a.shape[-1])
    feat_major = lambda c: jnp.transpose(c[0], (0, 2, 3, 1)).reshape(dec_batch, WIDTH, past)
    smix_a = _sample_attention("sb", per_b(sqa), per_b(ska), per_b(sva), per_b(sga),
                               feat_major(cache_a_k), feat_major(cache_a_v), tt, CHUNK)
    smix_b = _sample_attention("fox", per_b(sqb), per_b(skb), per_b(svb), per_b(sgb),
                               feat_major(cache_b_k), feat_major(cache_b_v), (per_b(scs), *bias_parts),
                               SAMPLE_CHUNK)
    y_sample = _output(smix_a.reshape(s_rows, WIDTH), smix_b.reshape(s_rows, WIDTH),
                       x_sample.reshape(s_rows, D_MODEL), wa, wb, g_out, 2 * tm).reshape(dec_batch, dec_seq, D_MODEL)

    def prompt_heads(at):
        return jnp.transpose(at.reshape(batch, HEADS, HEAD_DIM, length), (0, 3, 1, 2))[None]

    sample_heads = lambda a: a.reshape(1, dec_batch, dec_seq, HEADS, HEAD_DIM)
    return (y_prompt, y_sample,
            prompt_heads(kat), prompt_heads(vat), prompt_heads(kbt), prompt_heads(vbt),
            jnp.transpose(lft, (0, 2, 1))[None],
            sample_heads(ska), sample_heads(sva), sample_heads(skb), sample_heads(svb),
            slf.reshape(1, dec_batch, dec_seq, HEADS))
```

```python
import functools

import jax
import jax.numpy as jnp
import numpy as np
from jax import lax
from jax.experimental import pallas as pl
from jax.experimental.pallas import tpu as pltpu

F32 = jnp.float32
BF16 = jnp.bfloat16

D_MODEL = 1024
HEAD_DIM = 64
HEADS = 8
WIDTH = HEADS * HEAD_DIM
N_META = 16
EPS = 1e-6
LANES = 128
SUB = 128
CHUNK = 512
FOX_CHUNK = 1024
SAMPLE_CHUNK = 1024
Q_ROWS = 128
FOX_Q_ROWS = 256
LOG2_E = 1.4426950408889634
SB_DEAD_CARRY = 104.0
BIAS_ROWS = 32
PAD_KEYS = SUB - N_META
PROMPT_STREAMS = 2
NEG_BIG = -1e30
VMEM_LIMIT = 56 * 1024 * 1024


def _split3(x):
    hi = x.astype(BF16)
    r1 = x - hi.astype(F32)
    mid = r1.astype(BF16)
    lo = (r1 - mid.astype(F32)).astype(BF16)
    return hi, mid, lo


def _log_sigmoid(x):
    return jnp.minimum(x, 0.0) - jnp.log(1.0 + jnp.exp(-jnp.abs(x)))


def _silu(g):
    return g / (1.0 + jnp.exp(-g))


def _dot(a, b):
    return jnp.dot(a, b, preferred_element_type=F32)


def _dot_nt(a, b):
    return lax.dot_general(a, b, (((1,), (1,)), ((), ())), preferred_element_type=F32)


def _rms_normed(x, g):
    ms = jnp.mean(x * x, axis=-1, keepdims=True)
    return ((x * lax.rsqrt(ms + EPS)) * g).astype(BF16)


def _cache_bias_body(x_ref, w_ref, s_ref, carry_ref):
    @pl.when(pl.program_id(0) == 0)
    def _():
        carry_ref[...] = jnp.zeros_like(carry_ref)

    x = x_ref[...]
    tb = x.shape[1]
    hi, mid, lo = _split3(x)
    w = w_ref[...]
    r = _dot(hi, w) + _dot(mid, w) + _dot(lo, w)
    carry = carry_ref[...]
    s_ref[...] = r[:, :tb] + jnp.concatenate([carry] * (tb // LANES), axis=1)
    carry_ref[...] = carry + r[:, tb:]


def _cache_bias(x, w, tb):
    rows, past = x.shape
    nb = past // tb
    blk = pl.BlockSpec((rows, tb), lambda t: (0, nb - 1 - t))
    return pl.pallas_call(
        _cache_bias_body,
        grid=(nb,),
        in_specs=[blk, pl.BlockSpec(w.shape, lambda t: (0, 0))],
        out_specs=blk,
        out_shape=jax.ShapeDtypeStruct((rows, past), F32),
        scratch_shapes=[pltpu.VMEM((rows, LANES), F32)],
        compiler_params=pltpu.CompilerParams(dimension_semantics=("arbitrary",),
                                             vmem_limit_bytes=VMEM_LIMIT),
        name="cache_bias",
    )(x, w)


def _proj_prompt_body(x_ref, g_ref, wt_ref, wft_ref, bft_ref, triu_ref,
                      qa_ref, ga_ref, qb_ref, gb_ref, kat_ref, vat_ref, kbt_ref, vbt_ref,
                      lft_ref, cst_ref, carry_ref):
    t = pl.program_id(1)
    xb = _rms_normed(x_ref[0], g_ref[...])
    tm = xb.shape[0]
    slab = lambda c: wt_ref[c * WIDTH:(c + 1) * WIDTH, :]
    for c, o_ref in ((0, qa_ref), (3, ga_ref), (4, qb_ref), (7, gb_ref)):
        u = _dot_nt(xb, slab(c))
        if c % 4 == 0:
            u = u * (1.0 / 8.0)
        o_ref[0] = u.astype(o_ref.dtype)
    for c, o_ref in ((1, kat_ref), (2, vat_ref), (5, kbt_ref), (6, vbt_ref)):
        o_ref[0] = _dot_nt(slab(c), xb)

    reps = tm // LANES
    flt = _dot_nt(wft_ref[...], xb) + jnp.concatenate([bft_ref[...]] * reps, axis=1)
    logf = _log_sigmoid(flt)
    lft_ref[0] = logf[0:HEADS]
    hi, mid, lo = _split3(logf)
    triu = triu_ref[...]
    c = _dot(hi, triu) + _dot(mid, triu) + _dot(lo, triu)

    @pl.when(t == 0)
    def _():
        carry_ref[...] = jnp.zeros_like(carry_ref)

    c = c + jnp.concatenate([carry_ref[...]] * reps, axis=1)
    carry_ref[...] = jnp.broadcast_to(c[:, tm - 1:tm], carry_ref.shape)
    h, m, l = _split3(-c[0:HEADS])
    cst_ref[0] = jnp.concatenate([h.astype(F32), m.astype(F32), l.astype(F32),
                                  jnp.zeros((HEADS, tm), F32)], axis=0).astype(BF16)


def _project_prompt(hp, length, g, wt, wft, bft, triu, tm):
    b, rows_in, _ = hp.shape
    nt = rows_in // tm
    const = lambda s: pl.BlockSpec(s, lambda i, t: (0,) * len(s))
    tok = lambda w: pl.BlockSpec((1, tm, w), lambda i, t: (i, t, 0))
    feat = lambda r: pl.BlockSpec((1, r, tm), lambda i, t: (i, 0, t))
    out_shape = ([jax.ShapeDtypeStruct((b, length, WIDTH), dt) for dt in (BF16, F32, BF16, F32)]
                 + [jax.ShapeDtypeStruct((b, WIDTH, length), F32)] * 4
                 + [jax.ShapeDtypeStruct((b, HEADS, length), F32),
                    jax.ShapeDtypeStruct((b, BIAS_ROWS, length), BF16)])
    return pl.pallas_call(
        _proj_prompt_body,
        grid=(b, nt),
        in_specs=[tok(D_MODEL), const((1, D_MODEL)), const(wt.shape),
                  const(wft.shape), const(bft.shape), const(triu.shape)],
        out_specs=[tok(WIDTH)] * 4 + [feat(WIDTH)] * 4 + [feat(HEADS), feat(BIAS_ROWS)],
        out_shape=out_shape,
        scratch_shapes=[pltpu.VMEM((16, LANES), F32)],
        compiler_params=pltpu.CompilerParams(dimension_semantics=("arbitrary", "arbitrary"),
                                             vmem_limit_bytes=VMEM_LIMIT),
        name="project_prompt",
    )(hp, g, wt, wft, bft, triu)


def _proj_sample_body(x_ref, g_ref, wt_ref, wf_ref, bf_ref, tri_ref,
                      qa_ref, ka_ref, va_ref, ga_ref, qb_ref, kb_ref, vb_ref, gb_ref, lf_ref, cs_ref):
    xb = _rms_normed(x_ref[...], g_ref[...])
    tm = xb.shape[0]
    for c, o_ref in enumerate((qa_ref, ka_ref, va_ref, ga_ref, qb_ref, kb_ref, vb_ref, gb_ref)):
        u = _dot_nt(xb, wt_ref[c * WIDTH:(c + 1) * WIDTH, :])
        if c % 4 == 0:
            u = u * (1.0 / 8.0)
        o_ref[...] = u.astype(o_ref.dtype)
    logf = _log_sigmoid(_dot_nt(xb, wf_ref[...]) + bf_ref[...])
    lf_ref[...] = logf[:, 0:HEADS]
    lane = lax.broadcasted_iota(jnp.int32, (tm, LANES), 1)
    hi, mid, lo = _split3(jnp.where(lane < 3 * HEADS, logf, 0.0))
    tri = tri_ref[...]
    c = _dot(tri, hi) + _dot(tri, mid) + _dot(tri, lo)
    h, m, l = _split3(-c)
    zero = jnp.zeros_like(h)
    cs_ref[...] = jnp.where(lane < 8, h, jnp.where(lane < 16, m, jnp.where(lane < 24, l, zero)))


def _project_sample(x, g, wm, wf, bf, tri, tm):
    rows = x.shape[0]
    row = lambda w: pl.BlockSpec((tm, w), lambda t: (t, 0))
    const = lambda s: pl.BlockSpec(s, lambda t: (0,) * len(s))
    slab_dtypes = [BF16, F32, F32, F32, BF16, F32, F32, F32]
    out_shape = ([jax.ShapeDtypeStruct((rows, WIDTH), dt) for dt in slab_dtypes]
                 + [jax.ShapeDtypeStruct((rows, HEADS), F32), jax.ShapeDtypeStruct((rows, LANES), BF16)])
    return pl.pallas_call(
        _proj_sample_body,
        grid=(rows // tm,),
        in_specs=[row(D_MODEL), const((1, D_MODEL)), const(wm.shape), const(wf.shape),
                  const((1, LANES)), const((tm, tm))],
        out_specs=[row(WIDTH)] * 8 + [row(HEADS), row(LANES)],
        out_shape=out_shape,
        compiler_params=pltpu.CompilerParams(dimension_semantics=("arbitrary",),
                                             vmem_limit_bytes=VMEM_LIMIT),
        name="project_sample",
    )(x, g, wm, wf, bf, tri)


def _sb_suffix(z, sp, tt, carry, valid):
    hi = sp.astype(BF16)
    lo = (sp - hi.astype(F32)).astype(BF16)
    cs = _dot(jnp.concatenate([hi, lo], axis=1), tt)
    w = jnp.exp(z - cs[:, :SUB] - carry)
    if valid is not None:
        w = jnp.where(valid, w, 0.0)
    return w.astype(BF16), carry + cs[:, SUB:]


def _softplus(z, valid):
    sp = jnp.maximum(z, 0.0) + jnp.log(1.0 + jnp.exp2(jnp.abs(z) * (-LOG2_E)))
    if valid is not None:
        sp = jnp.where(valid, sp, 0.0)
    return sp


def _sb_chunk(qv, kt, vt, tt, carry, acc, newest_valid):
    z = _dot(qv, kt)
    n = z.shape[1] // SUB
    ws = []
    for u in reversed(range(n)):
        zu = z[:, u * SUB:(u + 1) * SUB]
        valid = newest_valid if u == n - 1 else None
        w, carry = _sb_suffix(zu, _softplus(zu, valid), tt, carry, valid)
        ws.append(w)
    acc = acc + _dot_nt(jnp.concatenate(ws[::-1], axis=1), vt)
    return carry, acc


def _sb_rows(qv, k, v, tt, carry, acc, valid):
    z = _dot_nt(qv, k)
    w, carry = _sb_suffix(z, _softplus(z, valid), tt, carry, valid)
    return carry, acc + _dot(w, v)


def _softmax_step(s, m, l, acc, valid, pv):
    if valid is not None:
        s = jnp.where(valid, s, NEG_BIG)
    m_new = jnp.maximum(m, jnp.max(s, axis=-1, keepdims=True))
    alpha = jnp.exp(m - m_new)
    p = jnp.exp(s - m_new)
    l = alpha * l + jnp.sum(p, axis=-1, keepdims=True)
    acc = alpha * acc + pv(p.astype(BF16))
    return m_new, l, acc


def _one_hot_bias_cols(head, shape):
    col = lax.broadcasted_iota(jnp.int32, shape, 1)
    hit = (col == head) | (col == head + HEADS) | (col == head + 2 * HEADS)
    return jnp.where(hit, 1.0, 0.0).astype(BF16)


def _shifted(src, stage, pad):
    r, length = src.shape
    tail = (length // LANES) * LANES
    stage[0:r, tail:] = jnp.zeros((r, stage.shape[1] - tail), F32)
    stage[0:r, 0:length] = src
    return pltpu.roll(stage[0:r, :], pad, axis=1)


def _prompt_geometry(chunk, q_rows):
    rows = 2 * q_rows
    row = lax.broadcasted_iota(jnp.int32, (rows, chunk), 0)
    col = lax.broadcasted_iota(jnp.int32, (rows, chunk), 1)
    qidx = row & (q_rows - 1)
    r128 = lax.broadcasted_iota(jnp.int32, (rows, LANES), 0)
    c128 = lax.broadcasted_iota(jnp.int32, (rows, LANES), 1)
    head_mask = (r128 < q_rows) == (c128 < HEAD_DIM)
    first_head_lanes = lax.broadcasted_iota(jnp.int32, (q_rows, LANES), 1) < HEAD_DIM
    return col, qidx, head_mask, first_head_lanes


def _prompt_q_loop(q_rows, n_qblocks, n_streams, q_ref, g_ref, o_ref, head_mask, first_head_lanes, attend,
                   period=1):
    def q_block(qi, phase):
        qs = pl.multiple_of(N_META + qi * q_rows, 16)
        q = q_ref[0, pl.ds(qs, q_rows), :]
        qvs = []
        for s in range(n_streams):
            qp = q[:, s * LANES:(s + 1) * LANES]
            qvs.append(jnp.where(head_mask, jnp.concatenate([qp, qp], axis=0), jnp.zeros((), BF16)))
        outs = [jnp.where(first_head_lanes, o[:q_rows], o[q_rows:]) for o in attend(qi, qvs, phase)]
        g = g_ref[0, pl.ds(qs, q_rows), :]
        os_ = pl.multiple_of(qi * q_rows, q_rows)
        o_ref[0, pl.ds(os_, q_rows), :] = (jnp.concatenate(outs, axis=1) * _silu(g)).astype(o_ref.dtype)

    def group(gi, _):
        for phase in range(period):
            q_block(gi * period + phase, phase)
        return 0

    lax.fori_loop(0, n_qblocks // period, group, 0)


def _chunk_plan(qi, chunk):
    blk = qi + 1
    per = chunk // SUB
    return blk // per, (blk % per) * SUB


def _chunk_ref(ref, r0, nrows, c, chunk):
    return ref[r0:r0 + nrows, pl.ds(pl.multiple_of(c * chunk, chunk), chunk)]


def _sb_prompt_body(q_ref, kt_ref, vt_ref, g_ref, tt_ref, o_ref, k16, v16, stage, *, n_qblocks, n_streams):
    for s in range(n_streams):
        rs = slice(s * LANES, (s + 1) * LANES)
        k16[rs, :] = _shifted(kt_ref[0, rs, :], stage, PAD_KEYS).astype(BF16)
        v16[rs, :] = _shifted(vt_ref[0, rs, :], stage, PAD_KEYS).astype(BF16)
    col, qidx, head_mask, first_head_lanes = _prompt_geometry(SUB, Q_ROWS)
    diag_valid = col < qidx
    tt = tt_ref[...]
    rows = 2 * Q_ROWS
    per = CHUNK // SUB

    def attend(qi, qvs, phase):
        own = (phase + 1) % per
        last = (qi + 1) // per
        first_width = (own + 1) * SUB

        def chunk(c, state, width, newest_valid):
            lanes = pl.ds(pl.multiple_of(c * CHUNK, CHUNK), width)
            return tuple(
                _sb_chunk(qvs[s], k16[s * LANES:(s + 1) * LANES, lanes], v16[s * LANES:(s + 1) * LANES, lanes],
                          tt, carry, acc, newest_valid)
                for s, (carry, acc) in enumerate(state))

        def live(state):
            low = state[0][0]
            for carry, _ in state[1:]:
                low = jnp.minimum(low, carry)
            return jnp.min(low) < SB_DEAD_CARRY

        zero = jnp.zeros((rows, LANES), F32)
        state = chunk(last, ((zero, zero),) * n_streams, first_width, diag_valid)

        def older(loop):
            j, _, state = loop
            state = chunk(last - 1 - j, state, CHUNK, None)
            return j + 1, live(state), state

        _, _, state = lax.while_loop(lambda loop: (loop[0] < last) & loop[1], older,
                                     (jnp.int32(0), live(state), state))
        return [acc for _, acc in state]

    _prompt_q_loop(Q_ROWS, n_qblocks, n_streams, q_ref, g_ref, o_ref, head_mask, first_head_lanes, attend,
                   period=per)


def _fox_prompt_body(q_ref, kt_ref, vt_ref, cst_ref, g_ref, o_ref, kc16, v16, stage, *, n_qblocks, n_streams):
    q_rows = FOX_Q_ROWS
    pad = q_rows - N_META
    bias = _shifted(cst_ref[0].astype(F32), stage, pad)
    brow = lax.broadcasted_iota(jnp.int32, bias.shape, 0)
    blane = lax.broadcasted_iota(jnp.int32, bias.shape, 1)
    bias = jnp.where((blane < pad) & (brow < HEADS), NEG_BIG, bias).astype(BF16)
    kc_rows = 2 * LANES
    for s in range(n_streams):
        rs = slice(s * LANES, (s + 1) * LANES)
        kc16[s * kc_rows:s * kc_rows + LANES, :] = _shifted(kt_ref[0, rs, :], stage, pad).astype(BF16)
        kc16[s * kc_rows + LANES:s * kc_rows + LANES + BIAS_ROWS, :] = bias
        kc16[s * kc_rows + LANES + BIAS_ROWS:(s + 1) * kc_rows, :] = jnp.zeros(
            (LANES - BIAS_ROWS, kc16.shape[1]), BF16)
        v16[rs, :] = _shifted(vt_ref[0, rs, :], stage, pad).astype(BF16)
    col, qidx, head_mask, first_head_lanes = _prompt_geometry(FOX_CHUNK, q_rows)
    rows = 2 * q_rows
    r128 = lax.broadcasted_iota(jnp.int32, (rows, LANES), 0)
    first_head = 2 * n_streams * pl.program_id(1) + jnp.where(r128 < q_rows, 0, 1)
    qones = [_one_hot_bias_cols(first_head + 2 * s, (rows, LANES)) for s in range(n_streams)]

    per = FOX_CHUNK // q_rows

    def attend(qi, qvs, phase):
        own = (phase + 1) % per
        last = (qi + 1) // per
        first_width = (own + 1) * q_rows
        qv2s = [jnp.concatenate([qv, qone], axis=1) for qv, qone in zip(qvs, qones)]

        def chunk(c, state, width, valid):
            lanes = pl.ds(pl.multiple_of(c * FOX_CHUNK, FOX_CHUNK), width)
            out = []
            for s, (m, l, acc) in enumerate(state):
                vt = v16[s * LANES:(s + 1) * LANES, lanes]
                sc = _dot(qv2s[s], kc16[s * kc_rows:(s + 1) * kc_rows, lanes])
                out.append(_softmax_step(sc, m, l, acc, valid, lambda p, vt=vt: _dot_nt(p, vt)))
            return tuple(out)

        init = (jnp.full((rows, 1), NEG_BIG, F32), jnp.zeros((rows, 1), F32), jnp.zeros((rows, LANES), F32))
        key = lax.broadcasted_iota(jnp.int32, (rows, first_width), 1) - own * q_rows
        query = lax.broadcasted_iota(jnp.int32, (rows, first_width), 0) & (q_rows - 1)
        state = chunk(last, (init,) * n_streams, first_width, key <= query)
        state = lax.fori_loop(0, last, lambda j, st: chunk(last - 1 - j, st, FOX_CHUNK, None), state)
        return [acc / l for _, l, acc in state]

    _prompt_q_loop(q_rows, n_qblocks, n_streams, q_ref, g_ref, o_ref, head_mask, first_head_lanes, attend,
                   period=per)


def _prompt_attention(kind, q, kt, vt, g, extra, seq):
    b, length, _ = q.shape
    q_rows = Q_ROWS if kind == "sb" else FOX_Q_ROWS
    n_qblocks = seq // q_rows
    chunk = CHUNK if kind == "sb" else FOX_CHUNK
    padded = ((q_rows - N_META + length + chunk - 1) // chunk) * chunk
    ns = PROMPT_STREAMS
    wide = ns * LANES
    tok = pl.BlockSpec((1, length, wide), lambda i, p: (i, 0, p))
    feat = pl.BlockSpec((1, wide, length), lambda i, p: (i, p, 0))
    out_spec = pl.BlockSpec((1, seq, wide), lambda i, p: (i, 0, p))
    stage = pltpu.VMEM((LANES, padded), F32)
    if kind == "sb":
        body = functools.partial(_sb_prompt_body, n_qblocks=n_qblocks, n_streams=ns)
        in_specs = [tok, feat, feat, tok, pl.BlockSpec(extra.shape, lambda i, p: (0, 0))]
        args = (q, kt, vt, g, extra)
        scratch = [pltpu.VMEM((wide, padded), BF16), pltpu.VMEM((wide, padded), BF16), stage]
    else:
        body = functools.partial(_fox_prompt_body, n_qblocks=n_qblocks, n_streams=ns)
        in_specs = [tok, feat, feat, pl.BlockSpec((1, BIAS_ROWS, length), lambda i, p: (i, 0, 0)), tok]
        args = (q, kt, vt, extra, g)
        scratch = [pltpu.VMEM((2 * wide, padded), BF16), pltpu.VMEM((wide, padded), BF16), stage]
    return pl.pallas_call(
        body,
        grid=(b, WIDTH // wide),
        in_specs=in_specs,
        out_specs=out_spec,
        out_shape=jax.ShapeDtypeStruct((b, seq, WIDTH), BF16),
        scratch_shapes=scratch,
        compiler_params=pltpu.CompilerParams(dimension_semantics=("arbitrary", "arbitrary"),
                                             vmem_limit_bytes=VMEM_LIMIT),
        name=kind + "_prompt",
    )(*args)


def _sample_setup(q_ref, qv_s, dec):
    q = q_ref[0]
    lane_head = lax.broadcasted_iota(jnp.int32, q.shape, 1) >> 6
    for h in range(HEADS):
        qv_s[h * dec:(h + 1) * dec, :] = jnp.where(lane_head == h, q, jnp.zeros((), BF16))


def _pad_rows_bf16(x, rows):
    x = x.astype(BF16)
    return jnp.concatenate([x, jnp.zeros((rows - x.shape[0], x.shape[1]), BF16)], axis=0)


def _gather_heads(acc, dec):
    lane_head = lax.broadcasted_iota(jnp.int32, (dec, WIDTH), 1) >> 6
    out = jnp.zeros((dec, WIDTH), F32)
    for h in range(HEADS):
        out = jnp.where(lane_head == h, acc[h * dec:(h + 1) * dec, :], out)
    return out


def _sb_sample_body(q_ref, kn_ref, vn_ref, g_ref, ckt_hbm, cvt_hbm, tt_ref, o_ref,
                    qv_s, kbuf, vbuf, sems, *, dec, tkb):
    b = pl.program_id(0)
    rows = HEADS * dec
    tt = tt_ref[...]
    nkb = ckt_hbm.shape[2] // tkb

    def block_copies(j):
        lanes = pl.ds(pl.multiple_of((nkb - 1 - j) * tkb, tkb), tkb)
        return (pltpu.make_async_copy(ckt_hbm.at[b, :, lanes], kbuf, sems.at[0]),
                pltpu.make_async_copy(cvt_hbm.at[b, :, lanes], vbuf, sems.at[1]))

    def start(j):
        for cp in block_copies(j):
            cp.start()

    start(0)
    _sample_setup(q_ref, qv_s, dec)
    qv = qv_s[...]
    row = lax.broadcasted_iota(jnp.int32, (rows, SUB), 0)
    col = lax.broadcasted_iota(jnp.int32, (rows, SUB), 1)
    carry, acc = _sb_rows(qv, _pad_rows_bf16(kn_ref[0], SUB), _pad_rows_bf16(vn_ref[0], SUB),
                          tt, jnp.zeros((rows, SUB), F32), jnp.zeros((rows, WIDTH), F32),
                          col < (row & (dec - 1)))

    def older(loop):
        j, carry, acc = loop
        for cp in block_copies(j):
            cp.wait()
        carry, acc = _sb_chunk(qv, kbuf[...].astype(BF16), vbuf[...].astype(BF16), tt, carry, acc, None)
        nxt = j + 1
        more = (nxt < nkb) & (jnp.min(carry) < SB_DEAD_CARRY)

        @pl.when(more)
        def _():
            start(nxt)

        return jnp.where(more, nxt, nkb), carry, acc

    _, _, acc = lax.while_loop(lambda loop: loop[0] < nkb, older, (jnp.int32(0), carry, acc))
    o_ref[0] = (_gather_heads(acc, dec) * _silu(g_ref[0])).astype(o_ref.dtype)


def _fox_sample_body(q_ref, kn_ref, vn_ref, csn_ref, g_ref, ckt_ref, cvt_ref, bias_ref,
                     o_ref, qv_s, m_s, l_s, acc_s, *, dec):
    j = pl.program_id(1)
    rows = HEADS * dec
    head = lax.broadcasted_iota(jnp.int32, (rows, 1), 0) >> (dec.bit_length() - 1)

    @pl.when(j == 0)
    def _():
        _sample_setup(q_ref, qv_s, dec)
        row = lax.broadcasted_iota(jnp.int32, (rows, SUB), 0)
        col = lax.broadcasted_iota(jnp.int32, (rows, SUB), 1)
        vn = _pad_rows_bf16(vn_ref[0], SUB)
        s = (_dot_nt(qv_s[...], _pad_rows_bf16(kn_ref[0], SUB))
             + _dot_nt(_one_hot_bias_cols(head, (rows, LANES)), _pad_rows_bf16(csn_ref[0], SUB)))
        m, l, acc = _softmax_step(s, jnp.full((rows, 1), NEG_BIG, F32), jnp.zeros((rows, 1), F32),
                                  jnp.zeros((rows, WIDTH), F32), col <= (row & (dec - 1)),
                                  lambda p: _dot(p, vn))
        m_s[...] = m
        l_s[...] = l
        acc_s[...] = acc

    tkb = ckt_ref.shape[2]
    bias = bias_ref[0]
    bias_rows = jnp.concatenate([jnp.broadcast_to(bias[h:h + 1, :], (dec, tkb)) for h in range(HEADS)], axis=0)
    vt = cvt_ref[0].astype(BF16)
    s = _dot(qv_s[...], ckt_ref[0].astype(BF16)) + bias_rows
    m, l, acc = _softmax_step(s, m_s[...], l_s[...], acc_s[...], None, lambda p: _dot_nt(p, vt))
    m_s[...] = m
    l_s[...] = l
    acc_s[...] = acc

    @pl.when(j == pl.num_programs(1) - 1)
    def _():
        o_ref[0] = (_gather_heads(acc / l, dec) * _silu(g_ref[0])).astype(o_ref.dtype)


def _sample_attention(kind, q, kn, vn, g, cache_kt, cache_vt, extra, tkb):
    b, dec, _ = q.shape
    past = cache_kt.shape[2]
    nkb = past // tkb
    rows = HEADS * dec
    new = lambda w: pl.BlockSpec((1, dec, w), lambda i, j: (i, 0, 0))
    old = lambda r: pl.BlockSpec((1, r, tkb), lambda i, j: (i, 0, nkb - 1 - j))
    if kind == "sb":
        per_stream = lambda w: pl.BlockSpec((1, dec, w), lambda i: (i, 0, 0))
        in_hbm = pl.BlockSpec(memory_space=pl.ANY)
        return pl.pallas_call(
            functools.partial(_sb_sample_body, dec=dec, tkb=tkb),
            grid=(b,),
            in_specs=[per_stream(WIDTH)] * 4 + [in_hbm, in_hbm, pl.BlockSpec(extra.shape, lambda i: (0, 0))],
            out_specs=per_stream(WIDTH),
            out_shape=jax.ShapeDtypeStruct((b, dec, WIDTH), BF16),
            scratch_shapes=[pltpu.VMEM((rows, WIDTH), BF16), pltpu.VMEM((WIDTH, tkb), F32),
                            pltpu.VMEM((WIDTH, tkb), F32), pltpu.SemaphoreType.DMA((2,))],
            compiler_params=pltpu.CompilerParams(dimension_semantics=("arbitrary",),
                                                 vmem_limit_bytes=VMEM_LIMIT),
            name="sb_sample",
        )(q, kn, vn, g, cache_kt, cache_vt, extra)
    else:
        csn, bias = extra
        body = functools.partial(_fox_sample_body, dec=dec)
        in_specs = [new(WIDTH), new(WIDTH), new(WIDTH), new(LANES), new(WIDTH), old(WIDTH), old(WIDTH),
                    old(HEADS)]
        args = (q, kn, vn, csn, g, cache_kt, cache_vt, bias)
        scratch = [pltpu.VMEM((rows, WIDTH), BF16), pltpu.VMEM((rows, 1), F32),
                   pltpu.VMEM((rows, 1), F32), pltpu.VMEM((rows, WIDTH), F32)]
    return pl.pallas_call(
        body,
        grid=(b, nkb),
        in_specs=in_specs,
        out_specs=new(WIDTH),
        out_shape=jax.ShapeDtypeStruct((b, dec, WIDTH), BF16),
        scratch_shapes=scratch,
        compiler_params=pltpu.CompilerParams(dimension_semantics=("arbitrary", "arbitrary"),
                                             vmem_limit_bytes=VMEM_LIMIT),
        name=kind + "_sample",
    )(*args)


def _out_body(ma_ref, mb_ref, x_ref, wa_ref, wb_ref, g_ref, y_ref):
    h = x_ref[...] + _dot(ma_ref[...], wa_ref[...]) + _dot(mb_ref[...], wb_ref[...])
    ms = jnp.mean(h * h, axis=-1, keepdims=True)
    y_ref[...] = (h * lax.rsqrt(ms + EPS)) * g_ref[...]


def _output(ma, mb, x, wa, wb, g, tm):
    r = x.shape[0]
    row = lambda w: pl.BlockSpec((tm, w), lambda i: (i, 0))
    const = lambda s: pl.BlockSpec(s, lambda i: (0, 0))
    return pl.pallas_call(
        _out_body,
        grid=(r // tm,),
        in_specs=[row(WIDTH), row(WIDTH), row(D_MODEL), const(wa.shape), const(wb.shape),
                  const((1, D_MODEL))],
        out_specs=row(D_MODEL),
        out_shape=jax.ShapeDtypeStruct((r, D_MODEL), F32),
        compiler_params=pltpu.CompilerParams(dimension_semantics=("arbitrary",),
                                             vmem_limit_bytes=VMEM_LIMIT),
        name="output",
    )(ma, mb, x, wa, wb, g)


def _tri(n, cmp, group=None):
    i = np.arange(n)
    t = cmp(i[:, None], i[None, :])
    if group is not None:
        t &= (i[:, None] // group) == (i[None, :] // group)
    return t.astype(np.float32)


def _suffix_sum_matrix():
    t = _tri(SUB, np.greater_equal)
    half = np.concatenate([t, np.ones_like(t)], axis=1)
    return jnp.asarray(np.concatenate([half, half], axis=0), dtype=BF16)


def kernel(x_prompt, x_sample, cache_a_k, cache_a_v, cache_b_k, cache_b_v, cache_b_logf,
           meta_tokens, norm_g, w_in, b_f, w_out, final_g):
    depth = norm_g.shape[0]
    assert depth == 1, "single-layer trunk only"
    batch, seq, _ = x_prompt.shape
    dec_batch, dec_seq, _ = x_sample.shape
    past = cache_a_k.shape[2]
    length = N_META + seq
    tm = 256

    wt_all = jnp.transpose(w_in[0]).astype(BF16)
    wt = wt_all[:8 * WIDTH]
    wf_rows = wt_all[8 * WIDTH:]
    wft = jnp.pad(wf_rows, ((0, 16 - HEADS), (0, 0)))
    bft = jnp.broadcast_to(jnp.pad(b_f[0], (0, 16 - HEADS))[:, None], (16, LANES)).astype(F32)
    wf = jnp.pad(jnp.tile(wf_rows, (3, 1)), ((0, LANES - 3 * HEADS), (0, 0)))
    bf = jnp.pad(jnp.tile(b_f[0], 3), (0, LANES - 3 * HEADS)).reshape(1, LANES).astype(F32)
    g_in = norm_g[0].reshape(1, D_MODEL)
    g_out = final_g.reshape(1, D_MODEL)
    wa = w_out[0, :WIDTH].astype(BF16)
    wb = w_out[0, WIDTH:].astype(BF16)
    tt = _suffix_sum_matrix()
    triu = jnp.asarray(_tri(tm, np.less_equal), dtype=BF16)
    tri_s = jnp.asarray(_tri(tm, np.greater_equal, group=dec_seq), dtype=BF16)
    suffix_w = jnp.asarray(np.concatenate([_tri(tm, np.greater), np.ones((tm, LANES), np.float32)],
                                          axis=1), dtype=BF16)

    rows_in = -(-length // tm) * tm
    meta = jnp.broadcast_to(meta_tokens[None].astype(x_prompt.dtype), (batch, N_META, D_MODEL))
    hp = jnp.concatenate([meta, x_prompt, jnp.zeros((batch, rows_in - length, D_MODEL), x_prompt.dtype)],
                         axis=1)
    (qa, ga, qb, gb, kat, vat, kbt, vbt, lft, cst) = _project_prompt(
        hp, length, g_in, wt, wft, bft, triu, tm)
    mixed_a = _prompt_attention("sb", qa, kat, vat, ga, tt, seq)
    mixed_b = _prompt_attention("fox", qb, kbt, vbt, gb, cst, seq)
    y_prompt = _output(mixed_a.reshape(batch * seq, WIDTH), mixed_b.reshape(batch * seq, WIDTH),
                       x_prompt.reshape(batch * seq, D_MODEL), wa, wb, g_out, 2 * tm).reshape(batch, seq, D_MODEL)

    s_rows = dec_batch * dec_seq
    (sqa, ska, sva, sga, sqb, skb, svb, sgb, slf, scs) = _project_sample(
        x_sample.reshape(s_rows, D_MODEL), g_in, wt, wf, bf, tri_s, tm)
    clft = jnp.transpose(cache_b_logf[0], (0, 2, 1)).reshape(dec_batch * HEADS, past)
    cache_bias = _cache_bias(clft, suffix_w, tm).reshape(dec_batch, HEADS, past)
    per_b = lambda a: a.reshape(dec_batch, dec_seq, a.shape[-1])
    feat_major = lambda c: jnp.transpose(c[0], (0, 2, 3, 1)).reshape(dec_batch, WIDTH, past)
    smix_a = _sample_attention("sb", per_b(sqa), per_b(ska), per_b(sva), per_b(sga),
                               feat_major(cache_a_k), feat_major(cache_a_v), tt, CHUNK)
    smix_b = _sample_attention("fox", per_b(sqb), per_b(skb), per_b(svb), per_b(sgb),
                               feat_major(cache_b_k), feat_major(cache_b_v), (per_b(scs), cache_bias),
                               SAMPLE_CHUNK)
    y_sample = _output(smix_a.reshape(s_rows, WIDTH), smix_b.reshape(s_rows, WIDTH),
                       x_sample.reshape(s_rows, D_MODEL), wa, wb, g_out, 2 * tm).reshape(dec_batch, dec_seq, D_MODEL)

    def prompt_heads(at):
        return jnp.transpose(at.reshape(batch, HEADS, HEAD_DIM, length), (0, 3, 1, 2))[None]

    sample_heads = lambda a: a.reshape(1, dec_batch, dec_seq, HEADS, HEAD_DIM)
    return (y_prompt, y_sample,
            prompt_heads(kat), prompt_heads(vat), prompt_heads(kbt), prompt_heads(vbt),
            jnp.transpose(lft, (0, 2, 1))[None],
            sample_heads(ska), sample_heads(sva), sample_heads(skb), sample_heads(svb),
            slf.reshape(1, dec_batch, dec_seq, HEADS))
```

```python
import functools

import jax
import jax.numpy as jnp
import numpy as np
from jax import lax
from jax.experimental import pallas as pl
from jax.experimental.pallas import tpu as pltpu

F32 = jnp.float32
BF16 = jnp.bfloat16

D_MODEL = 1024
HEAD_DIM = 64
HEADS = 8
WIDTH = HEADS * HEAD_DIM
N_META = 16
EPS = 1e-6
LANES = 128
SUB = 128
CHUNK = 512
FOX_CHUNK = 1024
SAMPLE_CHUNK = 2048
Q_ROWS = 128
FOX_Q_ROWS = 256
LOG2_E = 1.4426950408889634
SB_DEAD_CARRY = 104.0
BIAS_ROWS = 32
PAD_KEYS = SUB - N_META
PROMPT_STREAMS = 2
NEG_BIG = -1e30
VMEM_LIMIT = 56 * 1024 * 1024


def _split3(x):
    hi = x.astype(BF16)
    r1 = x - hi.astype(F32)
    mid = r1.astype(BF16)
    lo = (r1 - mid.astype(F32)).astype(BF16)
    return hi, mid, lo


def _log_sigmoid(x):
    return jnp.minimum(x, 0.0) - jnp.log(1.0 + jnp.exp(-jnp.abs(x)))


def _silu(g):
    return g / (1.0 + jnp.exp(-g))


def _dot(a, b):
    return jnp.dot(a, b, preferred_element_type=F32)


def _dot_nt(a, b):
    return lax.dot_general(a, b, (((1,), (1,)), ((), ())), preferred_element_type=F32)


def _rms_normed(x, g):
    ms = jnp.mean(x * x, axis=-1, keepdims=True)
    return ((x * lax.rsqrt(ms + EPS)) * g).astype(BF16)


def _cache_bias_body(x_ref, w_ref, s_ref, carry_ref):
    @pl.when(pl.program_id(0) == 0)
    def _():
        carry_ref[...] = jnp.zeros_like(carry_ref)

    x = x_ref[...]
    tb = x.shape[1]
    hi, mid, lo = _split3(x)
    w = w_ref[...]
    r = _dot(hi, w) + _dot(mid, w) + _dot(lo, w)
    carry = carry_ref[...]
    s_ref[...] = r[:, :tb] + jnp.concatenate([carry] * (tb // LANES), axis=1)
    carry_ref[...] = carry + r[:, tb:]


def _cache_bias(x, w, tb):
    rows, past = x.shape
    nb = past // tb
    blk = pl.BlockSpec((rows, tb), lambda t: (0, nb - 1 - t))
    return pl.pallas_call(
        _cache_bias_body,
        grid=(nb,),
        in_specs=[blk, pl.BlockSpec(w.shape, lambda t: (0, 0))],
        out_specs=blk,
        out_shape=jax.ShapeDtypeStruct((rows, past), F32),
        scratch_shapes=[pltpu.VMEM((rows, LANES), F32)],
        compiler_params=pltpu.CompilerParams(dimension_semantics=("arbitrary",),
                                             vmem_limit_bytes=VMEM_LIMIT),
        name="cache_bias",
    )(x, w)


def _proj_prompt_body(x_ref, meta_ref, g_ref, wt_ref, wft_ref, bft_ref, triu_ref,
                      qa_ref, ga_ref, qb_ref, gb_ref, kat_ref, vat_ref, kbt_ref, vbt_ref,
                      lft_ref, cst_ref, carry_ref, head_ref):
    t = pl.program_id(1)

    @pl.when(t == 0)
    def _():
        head_ref[...] = meta_ref[...]

    x_blk = x_ref[0]
    tm = x_blk.shape[0]
    body = jnp.where(t < pl.num_programs(1) - 1, x_blk[0:tm - N_META], 0.0)
    rows = jnp.concatenate([head_ref[...], body], axis=0)
    head_ref[...] = x_blk[tm - N_META:]
    xb = _rms_normed(rows, g_ref[...])
    slab = lambda c: wt_ref[c * WIDTH:(c + 1) * WIDTH, :]
    for c, o_ref in ((0, qa_ref), (3, ga_ref), (4, qb_ref), (7, gb_ref)):
        u = _dot_nt(xb, slab(c))
        if c % 4 == 0:
            u = u * (1.0 / 8.0)
        o_ref[0] = u.astype(o_ref.dtype)
    for c, o_ref in ((1, kat_ref), (2, vat_ref), (5, kbt_ref), (6, vbt_ref)):
        o_ref[0] = _dot_nt(slab(c), xb)

    reps = tm // LANES
    flt = _dot_nt(wft_ref[...], xb) + jnp.concatenate([bft_ref[...]] * reps, axis=1)
    logf = _log_sigmoid(flt)
    lft_ref[0] = logf[0:HEADS]
    hi, mid, lo = _split3(logf)
    triu = triu_ref[...]
    c = _dot(hi, triu) + _dot(mid, triu) + _dot(lo, triu)

    @pl.when(t == 0)
    def _():
        carry_ref[...] = jnp.zeros_like(carry_ref)

    c = c + jnp.concatenate([carry_ref[...]] * reps, axis=1)
    carry_ref[...] = jnp.broadcast_to(c[:, tm - 1:tm], carry_ref.shape)
    h, m, l = _split3(-c[0:HEADS])
    cst_ref[0] = jnp.concatenate([h.astype(F32), m.astype(F32), l.astype(F32),
                                  jnp.zeros((HEADS, tm), F32)], axis=0).astype(BF16)


def _project_prompt(x, meta, g, wt, wft, bft, triu, tm):
    b, seq, _ = x.shape
    assert seq % tm == 0
    length = N_META + seq
    nt = seq // tm + 1
    const = lambda s: pl.BlockSpec(s, lambda i, t: (0,) * len(s))
    tok = lambda w: pl.BlockSpec((1, tm, w), lambda i, t: (i, t, 0))
    x_spec = pl.BlockSpec((1, tm, D_MODEL), lambda i, t: (i, jnp.minimum(t, nt - 2), 0))
    feat = lambda r: pl.BlockSpec((1, r, tm), lambda i, t: (i, 0, t))
    out_shape = ([jax.ShapeDtypeStruct((b, length, WIDTH), dt) for dt in (BF16, F32, BF16, F32)]
                 + [jax.ShapeDtypeStruct((b, WIDTH, length), F32)] * 4
                 + [jax.ShapeDtypeStruct((b, HEADS, length), F32),
                    jax.ShapeDtypeStruct((b, BIAS_ROWS, length), BF16)])
    return pl.pallas_call(
        _proj_prompt_body,
        grid=(b, nt),
        in_specs=[x_spec, const(meta.shape), const((1, D_MODEL)), const(wt.shape),
                  const(wft.shape), const(bft.shape), const(triu.shape)],
        out_specs=[tok(WIDTH)] * 4 + [feat(WIDTH)] * 4 + [feat(HEADS), feat(BIAS_ROWS)],
        out_shape=out_shape,
        scratch_shapes=[pltpu.VMEM((16, LANES), F32), pltpu.VMEM((N_META, D_MODEL), F32)],
        compiler_params=pltpu.CompilerParams(dimension_semantics=("arbitrary", "arbitrary"),
                                             vmem_limit_bytes=VMEM_LIMIT),
        name="project_prompt",
    )(x, meta, g, wt, wft, bft, triu)


def _proj_sample_body(x_ref, g_ref, wt_ref, wf_ref, bf_ref, tri_ref,
                      qa_ref, ka_ref, va_ref, ga_ref, qb_ref, kb_ref, vb_ref, gb_ref, lf_ref, cs_ref):
    xb = _rms_normed(x_ref[...], g_ref[...])
    tm = xb.shape[0]
    for c, o_ref in enumerate((qa_ref, ka_ref, va_ref, ga_ref, qb_ref, kb_ref, vb_ref, gb_ref)):
        u = _dot_nt(xb, wt_ref[c * WIDTH:(c + 1) * WIDTH, :])
        if c % 4 == 0:
            u = u * (1.0 / 8.0)
        o_ref[...] = u.astype(o_ref.dtype)
    logf = _log_sigmoid(_dot_nt(xb, wf_ref[...]) + bf_ref[...])
    lf_ref[...] = logf[:, 0:HEADS]
    lane = lax.broadcasted_iota(jnp.int32, (tm, LANES), 1)
    hi, mid, lo = _split3(jnp.where(lane < 3 * HEADS, logf, 0.0))
    tri = tri_ref[...]
    c = _dot(tri, hi) + _dot(tri, mid) + _dot(tri, lo)
    h, m, l = _split3(-c)
    zero = jnp.zeros_like(h)
    cs_ref[...] = jnp.where(lane < 8, h, jnp.where(lane < 16, m, jnp.where(lane < 24, l, zero)))


def _project_sample(x, g, wm, wf, bf, tri, tm):
    rows = x.shape[0]
    row = lambda w: pl.BlockSpec((tm, w), lambda t: (t, 0))
    const = lambda s: pl.BlockSpec(s, lambda t: (0,) * len(s))
    slab_dtypes = [BF16, F32, F32, F32, BF16, F32, F32, F32]
    out_shape = ([jax.ShapeDtypeStruct((rows, WIDTH), dt) for dt in slab_dtypes]
                 + [jax.ShapeDtypeStruct((rows, HEADS), F32), jax.ShapeDtypeStruct((rows, LANES), BF16)])
    return pl.pallas_call(
        _proj_sample_body,
        grid=(rows // tm,),
        in_specs=[row(D_MODEL), const((1, D_MODEL)), const(wm.shape), const(wf.shape),
                  const((1, LANES)), const((tm, tm))],
        out_specs=[row(WIDTH)] * 8 + [row(HEADS), row(LANES)],
        out_shape=out_shape,
        compiler_params=pltpu.CompilerParams(dimension_semantics=("arbitrary",),
                                             vmem_limit_bytes=VMEM_LIMIT),
        name="project_sample",
    )(x, g, wm, wf, bf, tri)


def _sb_suffix(z, sp, tt, carry, valid):
    hi = sp.astype(BF16)
    lo = (sp - hi.astype(F32)).astype(BF16)
    cs = _dot(jnp.concatenate([hi, lo], axis=1), tt)
    w = jnp.exp(z - cs[:, :SUB] - carry)
    if valid is not None:
        w = jnp.where(valid, w, 0.0)
    return w.astype(BF16), carry + cs[:, SUB:]


def _softplus(z, valid):
    sp = jnp.maximum(z, 0.0) + jnp.log(1.0 + jnp.exp2(jnp.abs(z) * (-LOG2_E)))
    if valid is not None:
        sp = jnp.where(valid, sp, 0.0)
    return sp


def _sb_chunk(qv, kt, vt, tt, carry, acc, newest_valid):
    z = _dot(qv, kt)
    n = z.shape[1] // SUB
    ws = []
    for u in reversed(range(n)):
        zu = z[:, u * SUB:(u + 1) * SUB]
        valid = newest_valid if u == n - 1 else None
        w, carry = _sb_suffix(zu, _softplus(zu, valid), tt, carry, valid)
        ws.append(w)
    acc = acc + _dot_nt(jnp.concatenate(ws[::-1], axis=1), vt)
    return carry, acc


def _sb_rows(qv, k, v, tt, carry, acc, valid):
    z = _dot_nt(qv, k)
    w, carry = _sb_suffix(z, _softplus(z, valid), tt, carry, valid)
    return carry, acc + _dot(w, v)


def _softmax_step(s, m, l, acc, valid, pv):
    if valid is not None:
        s = jnp.where(valid, s, NEG_BIG)
    m_new = jnp.maximum(m, jnp.max(s, axis=-1, keepdims=True))
    alpha = jnp.exp(m - m_new)
    p = jnp.exp(s - m_new)
    l = alpha * l + jnp.sum(p, axis=-1, keepdims=True)
    acc = alpha * acc + pv(p.astype(BF16))
    return m_new, l, acc


def _one_hot_bias_cols(head, shape):
    col = lax.broadcasted_iota(jnp.int32, shape, 1)
    hit = (col == head) | (col == head + HEADS) | (col == head + 2 * HEADS)
    return jnp.where(hit, 1.0, 0.0).astype(BF16)


def _shifted(src, stage, pad):
    r, length = src.shape
    tail = (length // LANES) * LANES
    stage[0:r, tail:] = jnp.zeros((r, stage.shape[1] - tail), F32)
    stage[0:r, 0:length] = src
    return pltpu.roll(stage[0:r, :], pad, axis=1)


def _prompt_geometry(chunk, q_rows):
    rows = 2 * q_rows
    row = lax.broadcasted_iota(jnp.int32, (rows, chunk), 0)
    col = lax.broadcasted_iota(jnp.int32, (rows, chunk), 1)
    qidx = row & (q_rows - 1)
    r128 = lax.broadcasted_iota(jnp.int32, (rows, LANES), 0)
    c128 = lax.broadcasted_iota(jnp.int32, (rows, LANES), 1)
    head_mask = (r128 < q_rows) == (c128 < HEAD_DIM)
    first_head_lanes = lax.broadcasted_iota(jnp.int32, (q_rows, LANES), 1) < HEAD_DIM
    return col, qidx, head_mask, first_head_lanes


def _prompt_q_loop(q_rows, n_qblocks, n_streams, q_ref, g_ref, o_ref, head_mask, first_head_lanes, attend,
                   period=1):
    def q_block(qi, phase):
        qs = pl.multiple_of(N_META + qi * q_rows, 16)
        q = q_ref[0, pl.ds(qs, q_rows), :]
        qvs = []
        for s in range(n_streams):
            qp = q[:, s * LANES:(s + 1) * LANES]
            qvs.append(jnp.where(head_mask, jnp.concatenate([qp, qp], axis=0), jnp.zeros((), BF16)))
        outs = [jnp.where(first_head_lanes, o[:q_rows], o[q_rows:]) for o in attend(qi, qvs, phase)]
        g = g_ref[0, pl.ds(qs, q_rows), :]
        os_ = pl.multiple_of(qi * q_rows, q_rows)
        o_ref[0, pl.ds(os_, q_rows), :] = (jnp.concatenate(outs, axis=1) * _silu(g)).astype(o_ref.dtype)

    def group(gi, _):
        for phase in range(period):
            q_block(gi * period + phase, phase)
        return 0

    lax.fori_loop(0, n_qblocks // period, group, 0)


def _chunk_plan(qi, chunk):
    blk = qi + 1
    per = chunk // SUB
    return blk // per, (blk % per) * SUB


def _chunk_ref(ref, r0, nrows, c, chunk):
    return ref[r0:r0 + nrows, pl.ds(pl.multiple_of(c * chunk, chunk), chunk)]


def _sb_prompt_body(q_ref, kt_ref, vt_ref, g_ref, tt_ref, o_ref, k16, v16, stage, *, n_qblocks, n_streams):
    for s in range(n_streams):
        rs = slice(s * LANES, (s + 1) * LANES)
        k16[rs, :] = _shifted(kt_ref[0, rs, :], stage, PAD_KEYS).astype(BF16)
        v16[rs, :] = _shifted(vt_ref[0, rs, :], stage, PAD_KEYS).astype(BF16)
    col, qidx, head_mask, first_head_lanes = _prompt_geometry(SUB, Q_ROWS)
    diag_valid = col < qidx
    tt = tt_ref[...]
    rows = 2 * Q_ROWS
    per = CHUNK // SUB

    def attend(qi, qvs, phase):
        own = (phase + 1) % per
        last = (qi + 1) // per
        first_width = (own + 1) * SUB

        def chunk(c, state, width, newest_valid):
            lanes = pl.ds(pl.multiple_of(c * CHUNK, CHUNK), width)
            return tuple(
                _sb_chunk(qvs[s], k16[s * LANES:(s + 1) * LANES, lanes], v16[s * LANES:(s + 1) * LANES, lanes],
                          tt, carry, acc, newest_valid)
                for s, (carry, acc) in enumerate(state))

        def live(state):
            low = state[0][0]
            for carry, _ in state[1:]:
                low = jnp.minimum(low, carry)
            return jnp.min(low) < SB_DEAD_CARRY

        zero = jnp.zeros((rows, LANES), F32)
        state = chunk(last, ((zero, zero),) * n_streams, first_width, diag_valid)

        def older(loop):
            j, _, state = loop
            state = chunk(last - 1 - j, state, CHUNK, None)
            return j + 1, live(state), state

        _, _, state = lax.while_loop(lambda loop: (loop[0] < last) & loop[1], older,
                                     (jnp.int32(0), live(state), state))
        return [acc for _, acc in state]

    _prompt_q_loop(Q_ROWS, n_qblocks, n_streams, q_ref, g_ref, o_ref, head_mask, first_head_lanes, attend,
                   period=per)


def _fox_prompt_body(q_ref, kt_ref, vt_ref, cst_ref, g_ref, o_ref, kc16, v16, stage, *, n_qblocks, n_streams):
    q_rows = FOX_Q_ROWS
    pad = q_rows - N_META
    bias = _shifted(cst_ref[0].astype(F32), stage, pad)
    brow = lax.broadcasted_iota(jnp.int32, bias.shape, 0)
    blane = lax.broadcasted_iota(jnp.int32, bias.shape, 1)
    bias = jnp.where((blane < pad) & (brow < HEADS), NEG_BIG, bias).astype(BF16)
    kc_rows = 2 * LANES
    for s in range(n_streams):
        rs = slice(s * LANES, (s + 1) * LANES)
        kc16[s * kc_rows:s * kc_rows + LANES, :] = _shifted(kt_ref[0, rs, :], stage, pad).astype(BF16)
        kc16[s * kc_rows + LANES:s * kc_rows + LANES + BIAS_ROWS, :] = bias
        kc16[s * kc_rows + LANES + BIAS_ROWS:(s + 1) * kc_rows, :] = jnp.zeros(
            (LANES - BIAS_ROWS, kc16.shape[1]), BF16)
        v16[rs, :] = _shifted(vt_ref[0, rs, :], stage, pad).astype(BF16)
    col, qidx, head_mask, first_head_lanes = _prompt_geometry(FOX_CHUNK, q_rows)
    rows = 2 * q_rows
    r128 = lax.broadcasted_iota(jnp.int32, (rows, LANES), 0)
    first_head = 2 * n_streams * pl.program_id(1) + jnp.where(r128 < q_rows, 0, 1)
    qones = [_one_hot_bias_cols(first_head + 2 * s, (rows, LANES)) for s in range(n_streams)]

    per = FOX_CHUNK // q_rows

    def attend(qi, qvs, phase):
        own = (phase + 1) % per
        last = (qi + 1) // per
        first_width = (own + 1) * q_rows
        qv2s = [jnp.concatenate([qv, qone], axis=1) for qv, qone in zip(qvs, qones)]

        def chunk(c, state, width, valid):
            lanes = pl.ds(pl.multiple_of(c * FOX_CHUNK, FOX_CHUNK), width)
            out = []
            for s, (m, l, acc) in enumerate(state):
                vt = v16[s * LANES:(s + 1) * LANES, lanes]
                sc = _dot(qv2s[s], kc16[s * kc_rows:(s + 1) * kc_rows, lanes])
                out.append(_softmax_step(sc, m, l, acc, valid, lambda p, vt=vt: _dot_nt(p, vt)))
            return tuple(out)

        init = (jnp.full((rows, 1), NEG_BIG, F32), jnp.zeros((rows, 1), F32), jnp.zeros((rows, LANES), F32))
        key = lax.broadcasted_iota(jnp.int32, (rows, first_width), 1) - own * q_rows
        query = lax.broadcasted_iota(jnp.int32, (rows, first_width), 0) & (q_rows - 1)
        state = chunk(last, (init,) * n_streams, first_width, key <= query)
        state = lax.fori_loop(0, last, lambda j, st: chunk(last - 1 - j, st, FOX_CHUNK, None), state)
        return [acc / l for _, l, acc in state]

    _prompt_q_loop(q_rows, n_qblocks, n_streams, q_ref, g_ref, o_ref, head_mask, first_head_lanes, attend,
                   period=per)


def _prompt_attention(kind, q, kt, vt, g, extra, seq):
    b, length, _ = q.shape
    q_rows = Q_ROWS if kind == "sb" else FOX_Q_ROWS
    n_qblocks = seq // q_rows
    chunk = CHUNK if kind == "sb" else FOX_CHUNK
    padded = ((q_rows - N_META + length + chunk - 1) // chunk) * chunk
    ns = PROMPT_STREAMS
    wide = ns * LANES
    tok = pl.BlockSpec((1, length, wide), lambda i, p: (i, 0, p))
    feat = pl.BlockSpec((1, wide, length), lambda i, p: (i, p, 0))
    out_spec = pl.BlockSpec((1, seq, wide), lambda i, p: (i, 0, p))
    stage = pltpu.VMEM((LANES, padded), F32)
    if kind == "sb":
        body = functools.partial(_sb_prompt_body, n_qblocks=n_qblocks, n_streams=ns)
        in_specs = [tok, feat, feat, tok, pl.BlockSpec(extra.shape, lambda i, p: (0, 0))]
        args = (q, kt, vt, g, extra)
        scratch = [pltpu.VMEM((wide, padded), BF16), pltpu.VMEM((wide, padded), BF16), stage]
    else:
        body = functools.partial(_fox_prompt_body, n_qblocks=n_qblocks, n_streams=ns)
        in_specs = [tok, feat, feat, pl.BlockSpec((1, BIAS_ROWS, length), lambda i, p: (i, 0, 0)), tok]
        args = (q, kt, vt, extra, g)
        scratch = [pltpu.VMEM((2 * wide, padded), BF16), pltpu.VMEM((wide, padded), BF16), stage]
    return pl.pallas_call(
        body,
        grid=(b, WIDTH // wide),
        in_specs=in_specs,
        out_specs=out_spec,
        out_shape=jax.ShapeDtypeStruct((b, seq, WIDTH), BF16),
        scratch_shapes=scratch,
        compiler_params=pltpu.CompilerParams(dimension_semantics=("arbitrary", "arbitrary"),
                                             vmem_limit_bytes=VMEM_LIMIT),
        name=kind + "_prompt",
    )(*args)


def _sample_setup(q_ref, qv_s, dec):
    q = q_ref[0]
    lane_head = lax.broadcasted_iota(jnp.int32, q.shape, 1) >> 6
    for h in range(HEADS):
        qv_s[h * dec:(h + 1) * dec, :] = jnp.where(lane_head == h, q, jnp.zeros((), BF16))


def _pad_rows_bf16(x, rows):
    x = x.astype(BF16)
    return jnp.concatenate([x, jnp.zeros((rows - x.shape[0], x.shape[1]), BF16)], axis=0)


def _gather_heads(acc, dec):
    lane_head = lax.broadcasted_iota(jnp.int32, (dec, WIDTH), 1) >> 6
    out = jnp.zeros((dec, WIDTH), F32)
    for h in range(HEADS):
        out = jnp.where(lane_head == h, acc[h * dec:(h + 1) * dec, :], out)
    return out


def _sb_sample_body(q_ref, kn_ref, vn_ref, g_ref, ckt_hbm, cvt_hbm, tt_ref, o_ref,
                    qv_s, kbuf, vbuf, sems, *, dec, tkb):
    b = pl.program_id(0)
    rows = HEADS * dec
    tt = tt_ref[...]
    nkb = ckt_hbm.shape[2] // tkb

    def block_copies(j):
        lanes = pl.ds(pl.multiple_of((nkb - 1 - j) * tkb, tkb), tkb)
        return (pltpu.make_async_copy(ckt_hbm.at[b, :, lanes], kbuf, sems.at[0]),
                pltpu.make_async_copy(cvt_hbm.at[b, :, lanes], vbuf, sems.at[1]))

    def start(j):
        for cp in block_copies(j):
            cp.start()

    start(0)
    _sample_setup(q_ref, qv_s, dec)
    qv = qv_s[...]
    row = lax.broadcasted_iota(jnp.int32, (rows, SUB), 0)
    col = lax.broadcasted_iota(jnp.int32, (rows, SUB), 1)
    carry, acc = _sb_rows(qv, _pad_rows_bf16(kn_ref[0], SUB), _pad_rows_bf16(vn_ref[0], SUB),
                          tt, jnp.zeros((rows, SUB), F32), jnp.zeros((rows, WIDTH), F32),
                          col < (row & (dec - 1)))

    def older(loop):
        j, carry, acc = loop
        for cp in block_copies(j):
            cp.wait()
        carry, acc = _sb_chunk(qv, kbuf[...].astype(BF16), vbuf[...].astype(BF16), tt, carry, acc, None)
        nxt = j + 1
        more = (nxt < nkb) & (jnp.min(carry) < SB_DEAD_CARRY)

        @pl.when(more)
        def _():
            start(nxt)

        return jnp.where(more, nxt, nkb), carry, acc

    _, _, acc = lax.while_loop(lambda loop: loop[0] < nkb, older, (jnp.int32(0), carry, acc))
    o_ref[0] = (_gather_heads(acc, dec) * _silu(g_ref[0])).astype(o_ref.dtype)


def _fox_sample_body(q_ref, kn_ref, vn_ref, csn_ref, g_ref, ckt_ref, cvt_ref, bias_ref,
                     o_ref, qv_s, m_s, l_s, acc_s, *, dec):
    j = pl.program_id(1)
    rows = HEADS * dec
    head = lax.broadcasted_iota(jnp.int32, (rows, 1), 0) >> (dec.bit_length() - 1)

    @pl.when(j == 0)
    def _():
        _sample_setup(q_ref, qv_s, dec)
        row = lax.broadcasted_iota(jnp.int32, (rows, SUB), 0)
        col = lax.broadcasted_iota(jnp.int32, (rows, SUB), 1)
        vn = _pad_rows_bf16(vn_ref[0], SUB)
        s = (_dot_nt(qv_s[...], _pad_rows_bf16(kn_ref[0], SUB))
             + _dot_nt(_one_hot_bias_cols(head, (rows, LANES)), _pad_rows_bf16(csn_ref[0], SUB)))
        m, l, acc = _softmax_step(s, jnp.full((rows, 1), NEG_BIG, F32), jnp.zeros((rows, 1), F32),
                                  jnp.zeros((rows, WIDTH), F32), col <= (row & (dec - 1)),
                                  lambda p: _dot(p, vn))
        m_s[...] = m
        l_s[...] = l
        acc_s[...] = acc

    tkb = ckt_ref.shape[2]
    bias = bias_ref[0]
    bias_rows = jnp.concatenate([jnp.broadcast_to(bias[h:h + 1, :], (dec, tkb)) for h in range(HEADS)], axis=0)
    vt = cvt_ref[0].astype(BF16)
    s = _dot(qv_s[...], ckt_ref[0].astype(BF16)) + bias_rows
    m, l, acc = _softmax_step(s, m_s[...], l_s[...], acc_s[...], None, lambda p: _dot_nt(p, vt))
    m_s[...] = m
    l_s[...] = l
    acc_s[...] = acc

    @pl.when(j == pl.num_programs(1) - 1)
    def _():
        o_ref[0] = (_gather_heads(acc / l, dec) * _silu(g_ref[0])).astype(o_ref.dtype)


def _sample_attention(kind, q, kn, vn, g, cache_kt, cache_vt, extra, tkb):
    b, dec, _ = q.shape
    past = cache_kt.shape[2]
    nkb = past // tkb
    rows = HEADS * dec
    new = lambda w: pl.BlockSpec((1, dec, w), lambda i, j: (i, 0, 0))
    old = lambda r: pl.BlockSpec((1, r, tkb), lambda i, j: (i, 0, nkb - 1 - j))
    if kind == "sb":
        per_stream = lambda w: pl.BlockSpec((1, dec, w), lambda i: (i, 0, 0))
        in_hbm = pl.BlockSpec(memory_space=pl.ANY)
        return pl.pallas_call(
            functools.partial(_sb_sample_body, dec=dec, tkb=tkb),
            grid=(b,),
            in_specs=[per_stream(WIDTH)] * 4 + [in_hbm, in_hbm, pl.BlockSpec(extra.shape, lambda i: (0, 0))],
            out_specs=per_stream(WIDTH),
            out_shape=jax.ShapeDtypeStruct((b, dec, WIDTH), BF16),
            scratch_shapes=[pltpu.VMEM((rows, WIDTH), BF16), pltpu.VMEM((WIDTH, tkb), F32),
                            pltpu.VMEM((WIDTH, tkb), F32), pltpu.SemaphoreType.DMA((2,))],
            compiler_params=pltpu.CompilerParams(dimension_semantics=("arbitrary",),
                                                 vmem_limit_bytes=VMEM_LIMIT),
            name="sb_sample",
        )(q, kn, vn, g, cache_kt, cache_vt, extra)
    else:
        csn, bias = extra
        body = functools.partial(_fox_sample_body, dec=dec)
        in_specs = [new(WIDTH), new(WIDTH), new(WIDTH), new(LANES), new(WIDTH), old(WIDTH), old(WIDTH),
                    old(HEADS)]
        args = (q, kn, vn, csn, g, cache_kt, cache_vt, bias)
        scratch = [pltpu.VMEM((rows, WIDTH), BF16), pltpu.VMEM((rows, 1), F32),
                   pltpu.VMEM((rows, 1), F32), pltpu.VMEM((rows, WIDTH), F32)]
    return pl.pallas_call(
        body,
        grid=(b, nkb),
        in_specs=in_specs,
        out_specs=new(WIDTH),
        out_shape=jax.ShapeDtypeStruct((b, dec, WIDTH), BF16),
        scratch_shapes=scratch,
        compiler_params=pltpu.CompilerParams(dimension_semantics=("arbitrary", "arbitrary"),
                                             vmem_limit_bytes=VMEM_LIMIT),
        name=kind + "_sample",
    )(*args)


def _out_body(ma_ref, mb_ref, x_ref, wa_ref, wb_ref, g_ref, y_ref):
    h = x_ref[...] + _dot(ma_ref[...], wa_ref[...]) + _dot(mb_ref[...], wb_ref[...])
    ms = jnp.mean(h * h, axis=-1, keepdims=True)
    y_ref[...] = (h * lax.rsqrt(ms + EPS)) * g_ref[...]


def _output(ma, mb, x, wa, wb, g, tm):
    r = x.shape[0]
    row = lambda w: pl.BlockSpec((tm, w), lambda i: (i, 0))
    const = lambda s: pl.BlockSpec(s, lambda i: (0, 0))
    return pl.pallas_call(
        _out_body,
        grid=(r // tm,),
        in_specs=[row(WIDTH), row(WIDTH), row(D_MODEL), const(wa.shape), const(wb.shape),
                  const((1, D_MODEL))],
        out_specs=row(D_MODEL),
        out_shape=jax.ShapeDtypeStruct((r, D_MODEL), F32),
        compiler_params=pltpu.CompilerParams(dimension_semantics=("arbitrary",),
                                             vmem_limit_bytes=VMEM_LIMIT),
        name="output",
    )(ma, mb, x, wa, wb, g)


def _tri(n, cmp, group=None):
    i = np.arange(n)
    t = cmp(i[:, None], i[None, :])
    if group is not None:
        t &= (i[:, None] // group) == (i[None, :] // group)
    return t.astype(np.float32)


def _suffix_sum_matrix():
    t = _tri(SUB, np.greater_equal)
    half = np.concatenate([t, np.ones_like(t)], axis=1)
    return jnp.asarray(np.concatenate([half, half], axis=0), dtype=BF16)


def kernel(x_prompt, x_sample, cache_a_k, cache_a_v, cache_b_k, cache_b_v, cache_b_logf,
           meta_tokens, norm_g, w_in, b_f, w_out, final_g):
    depth = norm_g.shape[0]
    assert depth == 1, "single-layer trunk only"
    batch, seq, _ = x_prompt.shape
    dec_batch, dec_seq, _ = x_sample.shape
    past = cache_a_k.shape[2]
    length = N_META + seq
    tm = 256

    wt_all = jnp.transpose(w_in[0]).astype(BF16)
    wt = wt_all[:8 * WIDTH]
    wf_rows = wt_all[8 * WIDTH:]
    wft = jnp.pad(wf_rows, ((0, 16 - HEADS), (0, 0)))
    bft = jnp.broadcast_to(jnp.pad(b_f[0], (0, 16 - HEADS))[:, None], (16, LANES)).astype(F32)
    wf = jnp.pad(jnp.tile(wf_rows, (3, 1)), ((0, LANES - 3 * HEADS), (0, 0)))
    bf = jnp.pad(jnp.tile(b_f[0], 3), (0, LANES - 3 * HEADS)).reshape(1, LANES).astype(F32)
    g_in = norm_g[0].reshape(1, D_MODEL)
    g_out = final_g.reshape(1, D_MODEL)
    wa = w_out[0, :WIDTH].astype(BF16)
    wb = w_out[0, WIDTH:].astype(BF16)
    tt = _suffix_sum_matrix()
    triu = jnp.asarray(_tri(tm, np.less_equal), dtype=BF16)
    tri_s = jnp.asarray(_tri(tm, np.greater_equal, group=dec_seq), dtype=BF16)
    suffix_w = jnp.asarray(np.concatenate([_tri(tm, np.greater), np.ones((tm, LANES), np.float32)],
                                          axis=1), dtype=BF16)

    assert meta_tokens.shape == (N_META, D_MODEL)
    (qa, ga, qb, gb, kat, vat, kbt, vbt, lft, cst) = _project_prompt(
        x_prompt, meta_tokens.astype(x_prompt.dtype), g_in, wt, wft, bft, triu, tm)
    mixed_a = _prompt_attention("sb", qa, kat, vat, ga, tt, seq)
    mixed_b = _prompt_attention("fox", qb, kbt, vbt, gb, cst, seq)
    y_prompt = _output(mixed_a.reshape(batch * seq, WIDTH), mixed_b.reshape(batch * seq, WIDTH),
                       x_prompt.reshape(batch * seq, D_MODEL), wa, wb, g_out, 2 * tm).reshape(batch, seq, D_MODEL)

    s_rows = dec_batch * dec_seq
    (sqa, ska, sva, sga, sqb, skb, svb, sgb, slf, scs) = _project_sample(
        x_sample.reshape(s_rows, D_MODEL), g_in, wt, wf, bf, tri_s, tm)
    clft = jnp.transpose(cache_b_logf[0], (0, 2, 1)).reshape(dec_batch * HEADS, past)
    cache_bias = _cache_bias(clft, suffix_w, tm).reshape(dec_batch, HEADS, past)
    per_b = lambda a: a.reshape(dec_batch, dec_seq, a.shape[-1])
    feat_major = lambda c: jnp.transpose(c[0], (0, 2, 3, 1)).reshape(dec_batch, WIDTH, past)
    smix_a = _sample_attention("sb", per_b(sqa), per_b(ska), per_b(sva), per_b(sga),
                               feat_major(cache_a_k), feat_major(cache_a_v), tt, CHUNK)
    smix_b = _sample_attention("fox", per_b(sqb), per_b(skb), per_b(svb), per_b(sgb),
                               feat_major(cache_b_k), feat_major(cache_b_v), (per_b(scs), cache_bias),
                               SAMPLE_CHUNK)
    y_sample = _output(smix_a.reshape(s_rows, WIDTH), smix_b.reshape(s_rows, WIDTH),
                       x_sample.reshape(s_rows, D_MODEL), wa, wb, g_out, 2 * tm).reshape(dec_batch, dec_seq, D_MODEL)

    def prompt_heads(at):
        return jnp.transpose(at.reshape(batch, HEADS, HEAD_DIM, length), (0, 3, 1, 2))[None]

    sample_heads = lambda a: a.reshape(1, dec_batch, dec_seq, HEADS, HEAD_DIM)
    return (y_prompt, y_sample,
            prompt_heads(kat), prompt_heads(vat), prompt_heads(kbt), prompt_heads(vbt),
            jnp.transpose(lft, (0, 2, 1))[None],
            sample_heads(ska), sample_heads(sva), sample_heads(skb), sample_heads(svb),
            slf.reshape(1, dec_batch, dec_seq, HEADS))
```

```python
import functools

import jax
import jax.numpy as jnp
import numpy as np
from jax import lax
from jax.experimental import pallas as pl
from jax.experimental.pallas import tpu as pltpu

F32 = jnp.float32
BF16 = jnp.bfloat16

D_MODEL = 1024
HEAD_DIM = 64
HEADS = 8
WIDTH = HEADS * HEAD_DIM
N_META = 16
EPS = 1e-6
LANES = 128
BF16_ROWS = 16
HEAD_SHIFT = HEAD_DIM.bit_length() - 1
SUB = 128
CHUNK = 512
FOX_CHUNK = 1024
SAMPLE_CHUNK = 2048
Q_ROWS = 256
FOX_Q_ROWS = 256
LOG2_E = 1.4426950408889634
SB_DEAD_CARRY = 104.0
BIAS_ROWS = 32
PROMPT_STREAMS = 2
NEG_BIG = -1e30
VMEM_LIMIT = 56 * 1024 * 1024


def _split3(x):
    hi = x.astype(BF16)
    r1 = x - hi.astype(F32)
    mid = r1.astype(BF16)
    lo = (r1 - mid.astype(F32)).astype(BF16)
    return hi, mid, lo


def _log_sigmoid(x):
    return jnp.minimum(x, 0.0) - jnp.log(1.0 + jnp.exp(-jnp.abs(x)))


def _silu(g):
    return g / (1.0 + jnp.exp(-g))


def _dot(a, b):
    return jnp.dot(a, b, preferred_element_type=F32)


def _dot_nt(a, b):
    return lax.dot_general(a, b, (((1,), (1,)), ((), ())), preferred_element_type=F32)


def _rms_normed(x, g):
    ms = jnp.mean(x * x, axis=-1, keepdims=True)
    return ((x * lax.rsqrt(ms + EPS)) * g).astype(BF16)


def _cache_bias_body(x_ref, w_ref, s_ref, carry_ref):
    @pl.when(pl.program_id(0) == 0)
    def _():
        carry_ref[...] = jnp.zeros_like(carry_ref)

    x = x_ref[...]
    tb = x.shape[1]
    hi, mid, lo = _split3(x)
    w = w_ref[...]
    r = _dot(hi, w) + _dot(mid, w) + _dot(lo, w)
    carry = carry_ref[...]
    s_ref[...] = r[:, :tb] + jnp.concatenate([carry] * (tb // LANES), axis=1)
    carry_ref[...] = carry + r[:, tb:]


def _cache_bias(x, w, tb):
    rows, past = x.shape
    nb = past // tb
    blk = pl.BlockSpec((rows, tb), lambda t: (0, nb - 1 - t))
    return pl.pallas_call(
        _cache_bias_body,
        grid=(nb,),
        in_specs=[blk, pl.BlockSpec(w.shape, lambda t: (0, 0))],
        out_specs=blk,
        out_shape=jax.ShapeDtypeStruct((rows, past), F32),
        scratch_shapes=[pltpu.VMEM((rows, LANES), F32)],
        compiler_params=pltpu.CompilerParams(dimension_semantics=("arbitrary",),
                                             vmem_limit_bytes=VMEM_LIMIT),
        name="cache_bias",
    )(x, w)


def _proj_prompt_body(x_ref, meta_ref, g_ref, wt_ref, wft_ref, bft_ref, triu_ref,
                      qa_ref, ga_ref, qb_ref, gb_ref, kat_ref, vat_ref, kbt_ref, vbt_ref,
                      lft_ref, cst_ref, carry_ref, head_ref):
    t = pl.program_id(1)

    @pl.when(t == 0)
    def _():
        head_ref[...] = meta_ref[...]

    x_blk = x_ref[0]
    tm = x_blk.shape[0]
    body = jnp.where(t < pl.num_programs(1) - 1, x_blk[0:tm - N_META], 0.0)
    rows = jnp.concatenate([head_ref[...], body], axis=0)
    head_ref[...] = x_blk[tm - N_META:]
    xb = _rms_normed(rows, g_ref[...])
    slab = lambda c: wt_ref[c * WIDTH:(c + 1) * WIDTH, :]
    for c, o_ref in ((0, qa_ref), (3, ga_ref), (4, qb_ref), (7, gb_ref)):
        u = _dot_nt(xb, slab(c))
        if c % 4 == 0:
            u = u * (1.0 / 8.0)
        o_ref[0] = u.astype(o_ref.dtype)
    for c, o_ref in ((1, kat_ref), (2, vat_ref), (5, kbt_ref), (6, vbt_ref)):
        o_ref[0] = _dot_nt(slab(c), xb)

    reps = tm // LANES
    flt = _dot_nt(wft_ref[...], xb) + jnp.concatenate([bft_ref[...]] * reps, axis=1)
    logf = _log_sigmoid(flt)
    lft_ref[0] = logf[0:HEADS]
    hi, mid, lo = _split3(logf)
    triu = triu_ref[...]
    c = _dot(hi, triu) + _dot(mid, triu) + _dot(lo, triu)

    @pl.when(t == 0)
    def _():
        carry_ref[...] = jnp.zeros_like(carry_ref)

    c = c + jnp.concatenate([carry_ref[...]] * reps, axis=1)
    carry_ref[...] = jnp.broadcast_to(c[:, tm - 1:tm], carry_ref.shape)
    h, m, l = _split3(-c[0:HEADS])
    cst_ref[0] = jnp.concatenate([h.astype(F32), m.astype(F32), l.astype(F32),
                                  jnp.zeros((HEADS, tm), F32)], axis=0).astype(BF16)


def _project_prompt(x, meta, g, wt, wft, bft, triu, tm):
    b, seq, _ = x.shape
    assert seq % tm == 0
    length = N_META + seq
    nt = seq // tm + 1
    const = lambda s: pl.BlockSpec(s, lambda i, t: (0,) * len(s))
    tok = lambda w: pl.BlockSpec((1, tm, w), lambda i, t: (i, t, 0))
    x_spec = pl.BlockSpec((1, tm, D_MODEL), lambda i, t: (i, jnp.minimum(t, nt - 2), 0))
    feat = lambda r: pl.BlockSpec((1, r, tm), lambda i, t: (i, 0, t))
    out_shape = ([jax.ShapeDtypeStruct((b, length, WIDTH), dt) for dt in (BF16, F32, BF16, F32)]
                 + [jax.ShapeDtypeStruct((b, WIDTH, length), F32)] * 4
                 + [jax.ShapeDtypeStruct((b, HEADS, length), F32),
                    jax.ShapeDtypeStruct((b, BIAS_ROWS, length), BF16)])
    return pl.pallas_call(
        _proj_prompt_body,
        grid=(b, nt),
        in_specs=[x_spec, const(meta.shape), const((1, D_MODEL)), const(wt.shape),
                  const(wft.shape), const(bft.shape), const(triu.shape)],
        out_specs=[tok(WIDTH)] * 4 + [feat(WIDTH)] * 4 + [feat(HEADS), feat(BIAS_ROWS)],
        out_shape=out_shape,
        scratch_shapes=[pltpu.VMEM((BF16_ROWS, LANES), F32), pltpu.VMEM((N_META, D_MODEL), F32)],
        compiler_params=pltpu.CompilerParams(dimension_semantics=("arbitrary", "arbitrary"),
                                             vmem_limit_bytes=VMEM_LIMIT),
        name="project_prompt",
    )(x, meta, g, wt, wft, bft, triu)


def _proj_sample_body(x_ref, g_ref, wt_ref, wf_ref, bf_ref, tri_ref,
                      qa_ref, ka_ref, va_ref, ga_ref, qb_ref, kb_ref, vb_ref, gb_ref, lf_ref, cs_ref):
    xb = _rms_normed(x_ref[...], g_ref[...])
    tm = xb.shape[0]
    for c, o_ref in enumerate((qa_ref, ka_ref, va_ref, ga_ref, qb_ref, kb_ref, vb_ref, gb_ref)):
        u = _dot_nt(xb, wt_ref[c * WIDTH:(c + 1) * WIDTH, :])
        if c % 4 == 0:
            u = u * (1.0 / 8.0)
        o_ref[...] = u.astype(o_ref.dtype)
    logf = _log_sigmoid(_dot_nt(xb, wf_ref[...]) + bf_ref[...])
    lf_ref[...] = logf[:, 0:HEADS]
    lane = lax.broadcasted_iota(jnp.int32, (tm, LANES), 1)
    hi, mid, lo = _split3(jnp.where(lane < 3 * HEADS, logf, 0.0))
    tri = tri_ref[...]
    c = _dot(tri, hi) + _dot(tri, mid) + _dot(tri, lo)
    h, m, l = _split3(-c)
    zero = jnp.zeros_like(h)
    cs_ref[...] = jnp.where(lane < 8, h, jnp.where(lane < 16, m, jnp.where(lane < 24, l, zero)))


def _project_sample(x, g, wm, wf, bf, tri, tm):
    rows = x.shape[0]
    row = lambda w: pl.BlockSpec((tm, w), lambda t: (t, 0))
    const = lambda s: pl.BlockSpec(s, lambda t: (0,) * len(s))
    slab_dtypes = [BF16, F32, F32, F32, BF16, F32, F32, F32]
    out_shape = ([jax.ShapeDtypeStruct((rows, WIDTH), dt) for dt in slab_dtypes]
                 + [jax.ShapeDtypeStruct((rows, HEADS), F32), jax.ShapeDtypeStruct((rows, LANES), BF16)])
    return pl.pallas_call(
        _proj_sample_body,
        grid=(rows // tm,),
        in_specs=[row(D_MODEL), const((1, D_MODEL)), const(wm.shape), const(wf.shape),
                  const((1, LANES)), const((tm, tm))],
        out_specs=[row(WIDTH)] * 8 + [row(HEADS), row(LANES)],
        out_shape=out_shape,
        compiler_params=pltpu.CompilerParams(dimension_semantics=("arbitrary",),
                                             vmem_limit_bytes=VMEM_LIMIT),
        name="project_sample",
    )(x, g, wm, wf, bf, tri)


def _sb_suffix(z, sp, tt, carry, valid):
    hi = sp.astype(BF16)
    lo = (sp - hi.astype(F32)).astype(BF16)
    cs = _dot(jnp.concatenate([hi, lo], axis=1), tt)
    w = jnp.exp(z - cs[:, :SUB] - carry)
    if valid is not None:
        w = jnp.where(valid, w, 0.0)
    return w.astype(BF16), carry + cs[:, SUB:]


def _softplus(z, valid):
    sp = jnp.maximum(z, 0.0) + jnp.log(1.0 + jnp.exp2(jnp.abs(z) * (-LOG2_E)))
    if valid is not None:
        sp = jnp.where(valid, sp, 0.0)
    return sp


def _sb_chunk(qv, kt, vt, tt, carry, acc, newest_valid):
    z = _dot(qv, kt)
    n = z.shape[1] // SUB
    ws = []
    for u in reversed(range(n)):
        zu = z[:, u * SUB:(u + 1) * SUB]
        valid = newest_valid[n - 1 - u] if n - 1 - u < len(newest_valid) else None
        w, carry = _sb_suffix(zu, _softplus(zu, valid), tt, carry, valid)
        ws.append(w)
    acc = acc + _dot_nt(jnp.concatenate(ws[::-1], axis=1), vt)
    return carry, acc


def _sb_rows(qv, k, v, tt, carry, acc, valid):
    z = _dot_nt(qv, k)
    w, carry = _sb_suffix(z, _softplus(z, valid), tt, carry, valid)
    return carry, acc + _dot(w, v)


def _softmax_step(s, m, l, acc, valid, pv):
    if valid is not None:
        s = jnp.where(valid, s, NEG_BIG)
    m_new = jnp.maximum(m, jnp.max(s, axis=-1, keepdims=True))
    alpha = jnp.exp(m - m_new)
    p = jnp.exp(s - m_new)
    l = alpha * l + jnp.sum(p, axis=-1, keepdims=True)
    acc = alpha * acc + pv(p.astype(BF16))
    return m_new, l, acc


def _one_hot_bias_cols(head, shape):
    col = lax.broadcasted_iota(jnp.int32, shape, 1)
    hit = (col == head) | (col == head + HEADS) | (col == head + 2 * HEADS)
    return jnp.where(hit, 1.0, 0.0).astype(BF16)


def _shifted(src, stage, pad):
    r, length = src.shape
    tail = (length // LANES) * LANES
    stage[0:r, tail:] = jnp.zeros((r, stage.shape[1] - tail), F32)
    stage[0:r, 0:length] = src
    return pltpu.roll(stage[0:r, :], pad, axis=1)


def _prompt_geometry(chunk, q_rows):
    rows = 2 * q_rows
    row = lax.broadcasted_iota(jnp.int32, (rows, chunk), 0)
    col = lax.broadcasted_iota(jnp.int32, (rows, chunk), 1)
    qidx = row & (q_rows - 1)
    r128 = lax.broadcasted_iota(jnp.int32, (rows, LANES), 0)
    c128 = lax.broadcasted_iota(jnp.int32, (rows, LANES), 1)
    head_mask = (r128 < q_rows) == (c128 < HEAD_DIM)
    first_head_lanes = lax.broadcasted_iota(jnp.int32, (q_rows, LANES), 1) < HEAD_DIM
    return col, qidx, head_mask, first_head_lanes


def _prompt_q_loop(q_rows, n_qblocks, n_streams, q_ref, g_ref, o_ref, head_mask, first_head_lanes, attend,
                   period=1):
    def q_block(qi, phase):
        qs = pl.multiple_of(N_META + qi * q_rows, BF16_ROWS)
        q = q_ref[0, pl.ds(qs, q_rows), :]
        qvs = []
        for s in range(n_streams):
            qp = q[:, s * LANES:(s + 1) * LANES]
            qvs.append(jnp.where(head_mask, jnp.concatenate([qp, qp], axis=0), jnp.zeros((), BF16)))
        outs = [jnp.where(first_head_lanes, o[:q_rows], o[q_rows:]) for o in attend(qi, qvs, phase)]
        g = g_ref[0, pl.ds(qs, q_rows), :]
        os_ = pl.multiple_of(qi * q_rows, q_rows)
        o_ref[0, pl.ds(os_, q_rows), :] = (jnp.concatenate(outs, axis=1) * _silu(g)).astype(o_ref.dtype)

    def group(gi, _):
        for phase in range(period):
            q_block(gi * period + phase, phase)
        return 0

    lax.fori_loop(0, n_qblocks // period, group, 0)


def _sb_prompt_body(q_ref, kt_ref, vt_ref, g_ref, tt_ref, o_ref, k16, v16, stage, *, n_qblocks, n_streams):
    pad = Q_ROWS - N_META
    for s in range(n_streams):
        rs = slice(s * LANES, (s + 1) * LANES)
        k16[rs, :] = _shifted(kt_ref[0, rs, :], stage, pad).astype(BF16)
        v16[rs, :] = _shifted(vt_ref[0, rs, :], stage, pad).astype(BF16)
    col, qidx, head_mask, first_head_lanes = _prompt_geometry(SUB, Q_ROWS)
    diag_valid = [col + (Q_ROWS - SUB - b * SUB) < qidx for b in range(Q_ROWS // SUB)]
    tt = tt_ref[...]
    rows = 2 * Q_ROWS
    per = CHUNK // Q_ROWS

    def attend(qi, qvs, phase):
        own = (phase + 1) % per
        last = (qi + 1) // per
        first_width = (own + 1) * Q_ROWS

        def chunk(start, state, width, newest_valid):
            lanes = pl.ds(pl.multiple_of(start, Q_ROWS), width)
            return tuple(
                _sb_chunk(qvs[s], k16[s * LANES:(s + 1) * LANES, lanes], v16[s * LANES:(s + 1) * LANES, lanes],
                          tt, carry, acc, newest_valid)
                for s, (carry, acc) in enumerate(state))

        def live(state):
            low = state[0][0]
            for carry, _ in state[1:]:
                low = jnp.minimum(low, carry)
            return jnp.min(low) < SB_DEAD_CARRY

        zero = jnp.zeros((rows, LANES), F32)
        state = chunk(last * CHUNK, ((zero, zero),) * n_streams, first_width, diag_valid)

        n_older = last * per

        def older(loop):
            j, _, state = loop
            state = chunk((n_older - 1 - j) * Q_ROWS, state, Q_ROWS, ())
            return j + 1, live(state), state

        _, _, state = lax.while_loop(lambda loop: (loop[0] < n_older) & loop[1], older,
                                     (jnp.int32(0), live(state), state))
        return [acc for _, acc in state]

    _prompt_q_loop(Q_ROWS, n_qblocks, n_streams, q_ref, g_ref, o_ref, head_mask, first_head_lanes, attend,
                   period=per)


def _fox_prompt_body(q_ref, kt_ref, vt_ref, cst_ref, g_ref, o_ref, kc16, v16, stage, *, n_qblocks, n_streams):
    q_rows = FOX_Q_ROWS
    pad = q_rows - N_META
    bias = _shifted(cst_ref[0].astype(F32), stage, pad)
    brow = lax.broadcasted_iota(jnp.int32, bias.shape, 0)
    blane = lax.broadcasted_iota(jnp.int32, bias.shape, 1)
    bias = jnp.where((blane < pad) & (brow < HEADS), NEG_BIG, bias).astype(BF16)
    kc_rows = 2 * LANES
    for s in range(n_streams):
        rs = slice(s * LANES, (s + 1) * LANES)
        kc16[s * kc_rows:s * kc_rows + LANES, :] = _shifted(kt_ref[0, rs, :], stage, pad).astype(BF16)
        kc16[s * kc_rows + LANES:s * kc_rows + LANES + BIAS_ROWS, :] = bias
        kc16[s * kc_rows + LANES + BIAS_ROWS:(s + 1) * kc_rows, :] = jnp.zeros(
            (LANES - BIAS_ROWS, kc16.shape[1]), BF16)
        v16[rs, :] = _shifted(vt_ref[0, rs, :], stage, pad).astype(BF16)
    col, qidx, head_mask, first_head_lanes = _prompt_geometry(FOX_CHUNK, q_rows)
    rows = 2 * q_rows
    r128 = lax.broadcasted_iota(jnp.int32, (rows, LANES), 0)
    first_head = 2 * n_streams * pl.program_id(1) + jnp.where(r128 < q_rows, 0, 1)
    qones = [_one_hot_bias_cols(first_head + 2 * s, (rows, LANES)) for s in range(n_streams)]

    per = FOX_CHUNK // q_rows

    def attend(qi, qvs, phase):
        own = (phase + 1) % per
        last = (qi + 1) // per
        first_width = (own + 1) * q_rows
        qv2s = [jnp.concatenate([qv, qone], axis=1) for qv, qone in zip(qvs, qones)]

        def chunk(c, state, width, valid):
            lanes = pl.ds(pl.multiple_of(c * FOX_CHUNK, FOX_CHUNK), width)
            out = []
            for s, (m, l, acc) in enumerate(state):
                vt = v16[s * LANES:(s + 1) * LANES, lanes]
                sc = _dot(qv2s[s], kc16[s * kc_rows:(s + 1) * kc_rows, lanes])
                out.append(_softmax_step(sc, m, l, acc, valid, lambda p, vt=vt: _dot_nt(p, vt)))
            return tuple(out)

        init = (jnp.full((rows, 1), NEG_BIG, F32), jnp.zeros((rows, 1), F32), jnp.zeros((rows, LANES), F32))
        key = lax.broadcasted_iota(jnp.int32, (rows, first_width), 1) - own * q_rows
        query = lax.broadcasted_iota(jnp.int32, (rows, first_width), 0) & (q_rows - 1)
        state = chunk(last, (init,) * n_streams, first_width, key <= query)
        state = lax.fori_loop(0, last, lambda j, st: chunk(last - 1 - j, st, FOX_CHUNK, None), state)
        return [acc / l for _, l, acc in state]

    _prompt_q_loop(q_rows, n_qblocks, n_streams, q_ref, g_ref, o_ref, head_mask, first_head_lanes, attend,
                   period=per)


def _prompt_attention(kind, q, kt, vt, g, extra, seq):
    b, length, _ = q.shape
    q_rows = Q_ROWS if kind == "sb" else FOX_Q_ROWS
    n_qblocks = seq // q_rows
    chunk = CHUNK if kind == "sb" else FOX_CHUNK
    padded = ((q_rows - N_META + length + chunk - 1) // chunk) * chunk
    ns = PROMPT_STREAMS
    wide = ns * LANES
    tok = pl.BlockSpec((1, length, wide), lambda i, p: (i, 0, p))
    feat = pl.BlockSpec((1, wide, length), lambda i, p: (i, p, 0))
    out_spec = pl.BlockSpec((1, seq, wide), lambda i, p: (i, 0, p))
    stage = pltpu.VMEM((LANES, padded), F32)
    if kind == "sb":
        body = functools.partial(_sb_prompt_body, n_qblocks=n_qblocks, n_streams=ns)
        in_specs = [tok, feat, feat, tok, pl.BlockSpec(extra.shape, lambda i, p: (0, 0))]
        args = (q, kt, vt, g, extra)
        scratch = [pltpu.VMEM((wide, padded), BF16), pltpu.VMEM((wide, padded), BF16), stage]
    else:
        body = functools.partial(_fox_prompt_body, n_qblocks=n_qblocks, n_streams=ns)
        in_specs = [tok, feat, feat, pl.BlockSpec((1, BIAS_ROWS, length), lambda i, p: (i, 0, 0)), tok]
        args = (q, kt, vt, extra, g)
        scratch = [pltpu.VMEM((2 * wide, padded), BF16), pltpu.VMEM((wide, padded), BF16), stage]
    return pl.pallas_call(
        body,
        grid=(b, WIDTH // wide),
        in_specs=in_specs,
        out_specs=out_spec,
        out_shape=jax.ShapeDtypeStruct((b, seq, WIDTH), BF16),
        scratch_shapes=scratch,
        compiler_params=pltpu.CompilerParams(dimension_semantics=("arbitrary", "arbitrary"),
                                             vmem_limit_bytes=VMEM_LIMIT),
        name=kind + "_prompt",
    )(*args)


def _sample_setup(q_ref, qv_s, dec):
    q = q_ref[0]
    lane_head = lax.broadcasted_iota(jnp.int32, q.shape, 1) >> HEAD_SHIFT
    for h in range(HEADS):
        qv_s[h * dec:(h + 1) * dec, :] = jnp.where(lane_head == h, q, jnp.zeros((), BF16))


def _pad_rows_bf16(x, rows):
    x = x.astype(BF16)
    return jnp.concatenate([x, jnp.zeros((rows - x.shape[0], x.shape[1]), BF16)], axis=0)


def _gather_heads(acc, dec):
    lane_head = lax.broadcasted_iota(jnp.int32, (dec, WIDTH), 1) >> HEAD_SHIFT
    out = jnp.zeros((dec, WIDTH), F32)
    for h in range(HEADS):
        out = jnp.where(lane_head == h, acc[h * dec:(h + 1) * dec, :], out)
    return out


def _sb_sample_body(q_ref, kn_ref, vn_ref, g_ref, ckt_hbm, cvt_hbm, tt_ref, o_ref,
                    qv_s, kbuf, vbuf, sems, *, dec, tkb):
    b = pl.program_id(0)
    rows = HEADS * dec
    tt = tt_ref[...]
    nkb = ckt_hbm.shape[2] // tkb

    def block_copies(j):
        lanes = pl.ds(pl.multiple_of((nkb - 1 - j) * tkb, tkb), tkb)
        return (pltpu.make_async_copy(ckt_hbm.at[b, :, lanes], kbuf, sems.at[0]),
                pltpu.make_async_copy(cvt_hbm.at[b, :, lanes], vbuf, sems.at[1]))

    def start(j):
        for cp in block_copies(j):
            cp.start()

    start(0)
    _sample_setup(q_ref, qv_s, dec)
    qv = qv_s[...]
    row = lax.broadcasted_iota(jnp.int32, (rows, SUB), 0)
    col = lax.broadcasted_iota(jnp.int32, (rows, SUB), 1)
    carry, acc = _sb_rows(qv, _pad_rows_bf16(kn_ref[0], SUB), _pad_rows_bf16(vn_ref[0], SUB),
                          tt, jnp.zeros((rows, SUB), F32), jnp.zeros((rows, WIDTH), F32),
                          col < (row & (dec - 1)))

    def older(loop):
        j, carry, acc = loop
        for cp in block_copies(j):
            cp.wait()
        carry, acc = _sb_chunk(qv, kbuf[...].astype(BF16), vbuf[...].astype(BF16), tt, carry, acc, ())
        nxt = j + 1
        more = (nxt < nkb) & (jnp.min(carry) < SB_DEAD_CARRY)

        @pl.when(more)
        def _():
            start(nxt)

        return jnp.where(more, nxt, nkb), carry, acc

    _, _, acc = lax.while_loop(lambda loop: loop[0] < nkb, older, (jnp.int32(0), carry, acc))
    o_ref[0] = (_gather_heads(acc, dec) * _silu(g_ref[0])).astype(o_ref.dtype)


def _fox_sample_body(q_ref, kn_ref, vn_ref, csn_ref, g_ref, ckt_ref, cvt_ref, bias_ref,
                     o_ref, qv_s, m_s, l_s, acc_s, *, dec):
    j = pl.program_id(1)
    rows = HEADS * dec
    head = lax.broadcasted_iota(jnp.int32, (rows, 1), 0) >> (dec.bit_length() - 1)

    @pl.when(j == 0)
    def _():
        _sample_setup(q_ref, qv_s, dec)
        row = lax.broadcasted_iota(jnp.int32, (rows, SUB), 0)
        col = lax.broadcasted_iota(jnp.int32, (rows, SUB), 1)
        vn = _pad_rows_bf16(vn_ref[0], SUB)
        s = (_dot_nt(qv_s[...], _pad_rows_bf16(kn_ref[0], SUB))
             + _dot_nt(_one_hot_bias_cols(head, (rows, LANES)), _pad_rows_bf16(csn_ref[0], SUB)))
        m, l, acc = _softmax_step(s, jnp.full((rows, 1), NEG_BIG, F32), jnp.zeros((rows, 1), F32),
                                  jnp.zeros((rows, WIDTH), F32), col <= (row & (dec - 1)),
                                  lambda p: _dot(p, vn))
        m_s[...] = m
        l_s[...] = l
        acc_s[...] = acc

    tkb = ckt_ref.shape[2]
    bias = bias_ref[0]
    bias_rows = jnp.concatenate([jnp.broadcast_to(bias[h:h + 1, :], (dec, tkb)) for h in range(HEADS)], axis=0)
    vt = cvt_ref[0].astype(BF16)
    s = _dot(qv_s[...], ckt_ref[0].astype(BF16)) + bias_rows
    m, l, acc = _softmax_step(s, m_s[...], l_s[...], acc_s[...], None, lambda p: _dot_nt(p, vt))
    m_s[...] = m
    l_s[...] = l
    acc_s[...] = acc

    @pl.when(j == pl.num_programs(1) - 1)
    def _():
        o_ref[0] = (_gather_heads(acc / l, dec) * _silu(g_ref[0])).astype(o_ref.dtype)


def _sample_attention(kind, q, kn, vn, g, cache_kt, cache_vt, extra, tkb):
    b, dec, _ = q.shape
    past = cache_kt.shape[2]
    nkb = past // tkb
    rows = HEADS * dec
    new = lambda w: pl.BlockSpec((1, dec, w), lambda i, j: (i, 0, 0))
    old = lambda r: pl.BlockSpec((1, r, tkb), lambda i, j: (i, 0, nkb - 1 - j))
    if kind == "sb":
        per_stream = lambda w: pl.BlockSpec((1, dec, w), lambda i: (i, 0, 0))
        in_hbm = pl.BlockSpec(memory_space=pl.ANY)
        return pl.pallas_call(
            functools.partial(_sb_sample_body, dec=dec, tkb=tkb),
            grid=(b,),
            in_specs=[per_stream(WIDTH)] * 4 + [in_hbm, in_hbm, pl.BlockSpec(extra.shape, lambda i: (0, 0))],
            out_specs=per_stream(WIDTH),
            out_shape=jax.ShapeDtypeStruct((b, dec, WIDTH), BF16),
            scratch_shapes=[pltpu.VMEM((rows, WIDTH), BF16), pltpu.VMEM((WIDTH, tkb), F32),
                            pltpu.VMEM((WIDTH, tkb), F32), pltpu.SemaphoreType.DMA((2,))],
            compiler_params=pltpu.CompilerParams(dimension_semantics=("arbitrary",),
                                                 vmem_limit_bytes=VMEM_LIMIT),
            name="sb_sample",
        )(q, kn, vn, g, cache_kt, cache_vt, extra)
    else:
        csn, bias = extra
        body = functools.partial(_fox_sample_body, dec=dec)
        in_specs = [new(WIDTH), new(WIDTH), new(WIDTH), new(LANES), new(WIDTH), old(WIDTH), old(WIDTH),
                    old(HEADS)]
        args = (q, kn, vn, csn, g, cache_kt, cache_vt, bias)
        scratch = [pltpu.VMEM((rows, WIDTH), BF16), pltpu.VMEM((rows, 1), F32),
                   pltpu.VMEM((rows, 1), F32), pltpu.VMEM((rows, WIDTH), F32)]
    return pl.pallas_call(
        body,
        grid=(b, nkb),
        in_specs=in_specs,
        out_specs=new(WIDTH),
        out_shape=jax.ShapeDtypeStruct((b, dec, WIDTH), BF16),
        scratch_shapes=scratch,
        compiler_params=pltpu.CompilerParams(dimension_semantics=("arbitrary", "arbitrary"),
                                             vmem_limit_bytes=VMEM_LIMIT),
        name=kind + "_sample",
    )(*args)


def _out_body(ma_ref, mb_ref, x_ref, wa_ref, wb_ref, g_ref, y_ref):
    h = x_ref[...] + _dot(ma_ref[...], wa_ref[...]) + _dot(mb_ref[...], wb_ref[...])
    ms = jnp.mean(h * h, axis=-1, keepdims=True)
    y_ref[...] = (h * lax.rsqrt(ms + EPS)) * g_ref[...]


def _output(ma, mb, x, wa, wb, g, tm):
    r = x.shape[0]
    row = lambda w: pl.BlockSpec((tm, w), lambda i: (i, 0))
    const = lambda s: pl.BlockSpec(s, lambda i: (0, 0))
    return pl.pallas_call(
        _out_body,
        grid=(r // tm,),
        in_specs=[row(WIDTH), row(WIDTH), row(D_MODEL), const(wa.shape), const(wb.shape),
                  const((1, D_MODEL))],
        out_specs=row(D_MODEL),
        out_shape=jax.ShapeDtypeStruct((r, D_MODEL), F32),
        compiler_params=pltpu.CompilerParams(dimension_semantics=("arbitrary",),
                                             vmem_limit_bytes=VMEM_LIMIT),
        name="output",
    )(ma, mb, x, wa, wb, g)


def _tri(n, cmp, group=None):
    i = np.arange(n)
    t = cmp(i[:, None], i[None, :])
    if group is not None:
        t &= (i[:, None] // group) == (i[None, :] // group)
    return t.astype(np.float32)


def _suffix_sum_matrix():
    t = _tri(SUB, np.greater_equal)
    half = np.concatenate([t, np.ones_like(t)], axis=1)
    return jnp.asarray(np.concatenate([half, half], axis=0), dtype=BF16)


def kernel(x_prompt, x_sample, cache_a_k, cache_a_v, cache_b_k, cache_b_v, cache_b_logf,
           meta_tokens, norm_g, w_in, b_f, w_out, final_g):
    depth = norm_g.shape[0]
    assert depth == 1, "single-layer trunk only"
    batch, seq, _ = x_prompt.shape
    dec_batch, dec_seq, _ = x_sample.shape
    past = cache_a_k.shape[2]
    length = N_META + seq
    tm = 256

    wt_all = jnp.transpose(w_in[0]).astype(BF16)
    wt = wt_all[:8 * WIDTH]
    wf_rows = wt_all[8 * WIDTH:]
    wft = jnp.pad(wf_rows, ((0, BF16_ROWS - HEADS), (0, 0)))
    bft = jnp.broadcast_to(jnp.pad(b_f[0], (0, BF16_ROWS - HEADS))[:, None], (BF16_ROWS, LANES)).astype(F32)
    wf = jnp.pad(jnp.tile(wf_rows, (3, 1)), ((0, LANES - 3 * HEADS), (0, 0)))
    bf = jnp.pad(jnp.tile(b_f[0], 3), (0, LANES - 3 * HEADS)).reshape(1, LANES).astype(F32)
    g_in = norm_g[0].reshape(1, D_MODEL)
    g_out = final_g.reshape(1, D_MODEL)
    wa = w_out[0, :WIDTH].astype(BF16)
    wb = w_out[0, WIDTH:].astype(BF16)
    tt = _suffix_sum_matrix()
    tm_prompt = 2 * tm
    triu = jnp.asarray(_tri(tm_prompt, np.less_equal), dtype=BF16)
    tri_s = jnp.asarray(_tri(tm, np.greater_equal, group=dec_seq), dtype=BF16)
    suffix_w = jnp.asarray(np.concatenate([_tri(tm, np.greater), np.ones((tm, LANES), np.float32)],
                                          axis=1), dtype=BF16)

    assert meta_tokens.shape == (N_META, D_MODEL)
    (qa, ga, qb, gb, kat, vat, kbt, vbt, lft, cst) = _project_prompt(
        x_prompt, meta_tokens.astype(x_prompt.dtype), g_in, wt, wft, bft, triu, tm_prompt)
    mixed_a = _prompt_attention("sb", qa, kat, vat, ga, tt, seq)
    mixed_b = _prompt_attention("fox", qb, kbt, vbt, gb, cst, seq)
    y_prompt = _output(mixed_a.reshape(batch * seq, WIDTH), mixed_b.reshape(batch * seq, WIDTH),
                       x_prompt.reshape(batch * seq, D_MODEL), wa, wb, g_out, 2 * tm).reshape(batch, seq, D_MODEL)

    s_rows = dec_batch * dec_seq
    (sqa, ska, sva, sga, sqb, skb, svb, sgb, slf, scs) = _project_sample(
        x_sample.reshape(s_rows, D_MODEL), g_in, wt, wf, bf, tri_s, tm)
    clft = jnp.transpose(cache_b_logf[0], (0, 2, 1)).reshape(dec_batch * HEADS, past)
    cache_bias = _cache_bias(clft, suffix_w, tm).reshape(dec_batch, HEADS, past)
    per_b = lambda a: a.reshape(dec_batch, dec_seq, a.shape[-1])
    feat_major = lambda c: jnp.transpose(c[0], (0, 2, 3, 1)).reshape(dec_batch, WIDTH, past)
    smix_a = _sample_attention("sb", per_b(sqa), per_b(ska), per_b(sva), per_b(sga),
                               feat_major(cache_a_k), feat_major(cache_a_v), tt, CHUNK)
    smix_b = _sample_attention("fox", per_b(sqb), per_b(skb), per_b(svb), per_b(sgb),
                               feat_major(cache_b_k), feat_major(cache_b_v), (per_b(scs), cache_bias),
                               SAMPLE_CHUNK)
    y_sample = _output(smix_a.reshape(s_rows, WIDTH), smix_b.reshape(s_rows, WIDTH),
                       x_sample.reshape(s_rows, D_MODEL), wa, wb, g_out, 2 * tm).reshape(dec_batch, dec_seq, D_MODEL)

    def prompt_heads(at):
        return jnp.transpose(at.reshape(batch, HEADS, HEAD_DIM, length), (0, 3, 1, 2))[None]

    sample_heads = lambda a: a.reshape(1, dec_batch, dec_seq, HEADS, HEAD_DIM)
    return (y_prompt, y_sample,
            prompt_heads(kat), prompt_heads(vat), prompt_heads(kbt), prompt_heads(vbt),
            jnp.transpose(lft, (0, 2, 1))[None],
            sample_heads(ska), sample_heads(sva), sample_heads(skb), sample_heads(svb),
            slf.reshape(1, dec_batch, dec_seq, HEADS))
```
